```python
import math
import jax, jax.numpy as jnp
from jax import lax
import numpy as np

D_MODEL = 1024
BATCH = 2
SEQ = 8192
DEPTH = 2

GRID_W = 64
CTX_LEN = 256

MLA_HEADS = 8
MLA_NOPE = 64
MLA_ROPE = 32
MLA_QK = MLA_NOPE + MLA_ROPE
MLA_V = 64
MLA_Q_RANK = 768
MLA_KV_RANK = 256
MLA_WIDTH = MLA_HEADS * MLA_V
MLA_SCALE = 1.0 / math.sqrt(MLA_QK)
NA_HEADS = 4
NA_HEAD_DIM = 64
NA_WIDTH = NA_HEADS * NA_HEAD_DIM
NA_WIN_R = 8
NA_WIN_C = 16
NA_SCALE = 1.0 / math.sqrt(NA_HEAD_DIM)
POOL_WINDOWS = (2, 4, 8, 16)
POOL_GROUPS = 4
POOL_GROUP_DIM = 64
POOL_WIDTH = POOL_GROUPS * POOL_GROUP_DIM
MIX_WIDTH = MLA_WIDTH + NA_WIDTH + POOL_WIDTH
IN_COLS = MLA_Q_RANK + MLA_KV_RANK + MLA_ROPE + 3 * NA_WIDTH + POOL_WIDTH
D_FF = 2816
N_MOD = 9
ROPE_BASE = 10000.0
Q_BLOCK = 128
EPS = 1e-6

kernel_name = "hybrid_mla_natten_pool_macaron_dit"


def rms_norm(x, gain=None):
    xf = x.astype(jnp.float32)
    y = xf * lax.rsqrt(jnp.mean(xf * xf, axis=-1, keepdims=True) + EPS)
    if gain is not None:
        y = y * gain.astype(jnp.float32)
    return y.astype(x.dtype)


def modulate(h, shift, scale):
    return h * (1.0 + scale) + shift


def swiglu(h, w_gate, w_up, w_down):
    return (jax.nn.silu(h @ w_gate) * (h @ w_up)) @ w_down


def split_cols(z):
    sizes = (MLA_Q_RANK, MLA_KV_RANK, MLA_ROPE, NA_WIDTH, NA_WIDTH, NA_WIDTH, POOL_WIDTH)
    idx, acc = [], 0
    for s in sizes[:-1]:
        acc += s
        idx.append(acc)
    return jnp.split(z, idx, axis=-1)


def axial_rope_tables(n_tok):
    t = jnp.arange(n_tok, dtype=jnp.int32)
    row = (t // GRID_W).astype(jnp.float32)
    col = (t % GRID_W).astype(jnp.float32)
    axis_dim = MLA_ROPE // 2
    inv_freq = ROPE_BASE ** (-jnp.arange(0, axis_dim, 2, dtype=jnp.float32) / axis_dim)
    ar = row[:, None, None] * inv_freq
    ac = col[:, None, None] * inv_freq
    return (jnp.cos(ar), jnp.sin(ar), jnp.cos(ac), jnp.sin(ac))


def rotate(x, cos, sin):
    h = x.shape[-1] // 2
    x1, x2 = x[..., :h], x[..., h:]
    cos = cos.astype(x.dtype)
    sin = sin.astype(x.dtype)
    return jnp.concatenate([x1 * cos - x2 * sin, x1 * sin + x2 * cos], axis=-1)


def axial_rope(x, cos_r, sin_r, cos_c, sin_c):
    a = MLA_ROPE // 2
    return jnp.concatenate([rotate(x[..., :a], cos_r, sin_r), rotate(x[..., a:], cos_c, sin_c)], axis=-1)


def attend(q, k, v, scale):
    s = jnp.einsum('bqhd,bkhd->bhqk', q, k).astype(jnp.float32) * scale
    p = jax.nn.softmax(s, axis=-1).astype(v.dtype)
    return jnp.einsum('bhqk,bkhd->bqhd', p, v)


def blocked_attention(q, k, v, scale):
    B, N, H, Dk = q.shape
    nb = N // Q_BLOCK
    qb = q.reshape(B, nb, Q_BLOCK, H, Dk).transpose(1, 0, 2, 3, 4)
    out = lax.map(lambda qi: attend(qi, k, v, scale), qb)
    return out.transpose(1, 0, 2, 3, 4).reshape(B, N, H, v.shape[-1])


def mla_qkv(cq_in, ckv_in, kr_in, g_cq, w_q_up, g_ckv, w_kv_up, g_q, g_k, rope):
    B, N, _ = cq_in.shape
    q = (rms_norm(cq_in, g_cq) @ w_q_up).reshape(B, N, MLA_HEADS, MLA_QK)
    kv = (rms_norm(ckv_in, g_ckv) @ w_kv_up).reshape(B, N, MLA_HEADS, MLA_NOPE + MLA_V)
    k_nope, v = kv[..., :MLA_NOPE], kv[..., MLA_NOPE:]
    k_rope = jnp.broadcast_to(kr_in[:, :, None, :], (B, N, MLA_HEADS, MLA_ROPE))
    k = jnp.concatenate([k_nope, k_rope], axis=-1)
    q = rms_norm(q, g_q)
    k = rms_norm(k, g_k)
    if rope is not None:
        q = jnp.concatenate([q[..., :MLA_NOPE], axial_rope(q[..., MLA_NOPE:], *rope)], axis=-1)
        k = jnp.concatenate([k[..., :MLA_NOPE], axial_rope(k[..., MLA_NOPE:], *rope)], axis=-1)
    return q, k, v


def na_qkv(q_in, k_in, v_in, g_q, g_k):
    B, N, _ = q_in.shape
    shp = (B, N, NA_HEADS, NA_HEAD_DIM)
    return (rms_norm(q_in.reshape(shp), g_q), rms_norm(k_in.reshape(shp), g_k), v_in.reshape(shp))


def neighborhood_attention(q, k, v, k_ctx, v_ctx, rpb, rows):
    B, S, H, D = q.shape
    kr = min(NA_WIN_R, rows)
    kc = NA_WIN_C
    qg = q.reshape(B, rows, GRID_W, H, D)
    kg = k.reshape(B, rows, GRID_W, H, D)
    vg = v.reshape(B, rows, GRID_W, H, D)
    col = jnp.arange(GRID_W)
    col_start = jnp.clip(col - kc // 2, 0, GRID_W - kc)
    col_idx = col_start[:, None] + jnp.arange(kc)
    col_off = col_idx - col[:, None] + (NA_WIN_C - 1)

    def one_row(r):
        r0 = jnp.clip(r - kr // 2, 0, rows - kr)
        kblk = lax.dynamic_slice_in_dim(kg, r0, kr, axis=1)
        vblk = lax.dynamic_slice_in_dim(vg, r0, kr, axis=1)
        kn = kblk[:, :, col_idx]
        vn = vblk[:, :, col_idx]
        qr = lax.dynamic_index_in_dim(qg, r, axis=1, keepdims=False)
        row_off = r0 + jnp.arange(kr) - r + (NA_WIN_R - 1)
        bias = rpb[:, row_off[None, :, None], col_off[:, None, :]]
        s_loc = jnp.einsum('bwhd,brwkhd->bhwrk', qr, kn).astype(jnp.float32) * NA_SCALE
        s_loc = (s_loc + bias.astype(jnp.float32)).reshape(B, H, GRID_W, kr * kc)
        s_ctx = jnp.einsum('bwhd,bchd->bhwc', qr, k_ctx).astype(jnp.float32) * NA_SCALE
        p = jax.nn.softmax(jnp.concatenate([s_loc, s_ctx], axis=-1), axis=-1).astype(v.dtype)
        p_loc = p[..., :kr * kc].reshape(B, H, GRID_W, kr, kc)
        p_ctx = p[..., kr * kc:]
        return (jnp.einsum('bhwrk,brwkhd->bwhd', p_loc, vn)
                + jnp.einsum('bhwc,bchd->bwhd', p_ctx, v_ctx))

    out = lax.map(one_row, jnp.arange(rows))
    return out.transpose(1, 0, 2, 3, 4).reshape(B, S, H, D)


def multiscale_pool(u, w_pool, pool_scale):
    B, N, _ = u.shape
    uf = u.astype(jnp.float32)
    cs = jnp.concatenate([jnp.zeros((B, 1, POOL_WIDTH), jnp.float32), jnp.cumsum(uf, axis=1)], axis=1)
    t = jnp.arange(N)
    outs = []
    for g, w in enumerate(POOL_WINDOWS):
        lo = jnp.clip(t - w // 2, 0, N)
        hi = jnp.clip(t + w // 2, 0, N)
        seg = cs[:, :, g * POOL_GROUP_DIM:(g + 1) * POOL_GROUP_DIM]
        cnt = (hi - lo).astype(jnp.float32)[None, :, None]
        outs.append((seg[:, hi] - seg[:, lo]) / cnt - uf[..., g * POOL_GROUP_DIM:(g + 1) * POOL_GROUP_DIM])
    pooled = jnp.stack(outs, axis=2).astype(u.dtype)
    y = jnp.einsum('bngc,gcd->bngd', pooled, w_pool)
    return y.reshape(B, N, POOL_WIDTH) * pool_scale


def token_mixing(hx, hc, w_in, g_cq, w_q_up, g_ckv, w_kv_up, g_mla_q, g_mla_k,
                 g_na_q, g_na_k, na_rpb, w_pool, pool_scale, w_out, rope, rows, need_ctx):
    B, S, _ = hx.shape
    C = hc.shape[1]
    zx = split_cols(hx @ w_in)
    zc = split_cols(hc @ w_in)
    qa_x, ka_x, va_x = mla_qkv(zx[0], zx[1], zx[2], g_cq, w_q_up, g_ckv, w_kv_up, g_mla_q, g_mla_k, rope)
    qa_c, ka_c, va_c = mla_qkv(zc[0], zc[1], zc[2], g_cq, w_q_up, g_ckv, w_kv_up, g_mla_q, g_mla_k, None)
    ya_x = blocked_attention(qa_x, jnp.concatenate([ka_c, ka_x], axis=1),
                             jnp.concatenate([va_c, va_x], axis=1), MLA_SCALE).reshape(B, S, MLA_WIDTH)
    qb_x, kb_x, vb_x = na_qkv(zx[3], zx[4], zx[5], g_na_q, g_na_k)
    qb_c, kb_c, vb_c = na_qkv(zc[3], zc[4], zc[5], g_na_q, g_na_k)
    yb_x = neighborhood_attention(qb_x, kb_x, vb_x, kb_c, vb_c, na_rpb, rows).reshape(B, S, NA_WIDTH)
    yc_x = multiscale_pool(zx[6], w_pool, pool_scale)
    out_x = jnp.concatenate([ya_x, yb_x, yc_x], axis=-1) @ w_out
    if not need_ctx:
        return out_x, None
    ya_c = attend(qa_c, ka_c, va_c, MLA_SCALE).reshape(B, C, MLA_WIDTH)
    yb_c = attend(qb_c, kb_c, vb_c, NA_SCALE).reshape(B, C, NA_WIDTH)
    yc_c = multiscale_pool(zc[6], w_pool, pool_scale)
    out_c = jnp.concatenate([ya_c, yb_c, yc_c], axis=-1) @ w_out
    return out_x, out_c


def setup_inputs(seed: int = 0) -> dict:
    key = jax.random.key(seed)
    ks = jax.random.split(key, 24)
    f32 = jnp.float32
    L = DEPTH

    def nrm(k, shape, s):
        return jax.random.normal(k, shape, f32) * s

    return {
        "x": nrm(ks[0], (BATCH, SEQ, D_MODEL), 1.0),
        "c": nrm(ks[1], (BATCH, D_MODEL), 1.0),
        "ctx": nrm(ks[2], (BATCH, CTX_LEN, D_MODEL), 1.0),
        "c_ctx": nrm(ks[3], (D_MODEL,), 1.0),
        "w_mod": nrm(ks[4], (L, D_MODEL, N_MOD * D_MODEL), 0.5 * D_MODEL ** -0.5),
        "b_mod": nrm(ks[5], (L, N_MOD * D_MODEL), 0.02),
        "w_ffn_gate": nrm(ks[6], (L, 2, D_MODEL, D_FF), D_MODEL ** -0.5),
        "w_ffn_up": nrm(ks[7], (L, 2, D_MODEL, D_FF), D_MODEL ** -0.5),
        "w_ffn_down": nrm(ks[8], (L, 2, D_FF, D_MODEL), D_FF ** -0.5),
        "w_in": nrm(ks[9], (L, D_MODEL, IN_COLS), D_MODEL ** -0.5),
        "g_cq": 1.0 + nrm(ks[10], (L, MLA_Q_RANK), 0.05),
        "w_q_up": nrm(ks[11], (L, MLA_Q_RANK, MLA_HEADS * MLA_QK), MLA_Q_RANK ** -0.5),
        "g_ckv": 1.0 + nrm(ks[12], (L, MLA_KV_RANK), 0.05),
        "w_kv_up": nrm(ks[13], (L, MLA_KV_RANK, MLA_HEADS * (MLA_NOPE + MLA_V)), MLA_KV_RANK ** -0.5),
        "g_mla_q": 1.0 + nrm(ks[14], (L, MLA_QK), 0.05),
        "g_mla_k": 1.0 + nrm(ks[15], (L, MLA_QK), 0.05),
        "g_na_q": 1.0 + nrm(ks[16], (L, NA_HEAD_DIM), 0.05),
        "g_na_k": 1.0 + nrm(ks[17], (L, NA_HEAD_DIM), 0.05),
        "na_rpb": nrm(ks[18], (L, NA_HEADS, 2 * NA_WIN_R - 1, 2 * NA_WIN_C - 1), 0.1),
        "w_pool": nrm(ks[19], (L, POOL_GROUPS, POOL_GROUP_DIM, POOL_GROUP_DIM), POOL_GROUP_DIM ** -0.5),
        "pool_scale": 1.0 + nrm(ks[20], (L, POOL_WIDTH), 0.05),
        "w_out": nrm(ks[21], (L, MIX_WIDTH, D_MODEL), MIX_WIDTH ** -0.5),
    }


def reference(x, c, ctx, c_ctx, w_mod, b_mod, w_ffn_gate, w_ffn_up, w_ffn_down, w_in,
              g_cq, w_q_up, g_ckv, w_kv_up, g_mla_q, g_mla_k, g_na_q, g_na_k, na_rpb,
              w_pool, pool_scale, w_out):
    B, S, D = x.shape
    rows = S // GRID_W
    rope = axial_rope_tables(S)
    xc = ctx
    for l in range(DEPTH):
        last = l == DEPTH - 1
        mx = (jax.nn.silu(c) @ w_mod[l] + b_mod[l]).reshape(B, N_MOD, 1, D)
        mc = (jax.nn.silu(c_ctx) @ w_mod[l] + b_mod[l]).reshape(1, N_MOD, 1, D)
        x = x + 0.5 * mx[:, 2] * swiglu(modulate(rms_norm(x), mx[:, 0], mx[:, 1]),
                                        w_ffn_gate[l, 0], w_ffn_up[l, 0], w_ffn_down[l, 0])
        xc = xc + 0.5 * mc[:, 2] * swiglu(modulate(rms_norm(xc), mc[:, 0], mc[:, 1]),
                                          w_ffn_gate[l, 0], w_ffn_up[l, 0], w_ffn_down[l, 0])
        yx, yc = token_mixing(modulate(rms_norm(x), mx[:, 3], mx[:, 4]),
                              modulate(rms_norm(xc), mc[:, 3], mc[:, 4]),
                              w_in[l], g_cq[l], w_q_up[l], g_ckv[l], w_kv_up[l], g_mla_q[l], g_mla_k[l],
                              g_na_q[l], g_na_k[l], na_rpb[l], w_pool[l], pool_scale[l], w_out[l],
                              rope, rows, not last)
        x = x + mx[:, 5] * yx
        x = x + 0.5 * mx[:, 8] * swiglu(modulate(rms_norm(x), mx[:, 6], mx[:, 7]),
                                        w_ffn_gate[l, 1], w_ffn_up[l, 1], w_ffn_down[l, 1])
        if not last:
            xc = xc + mc[:, 5] * yc
            xc = xc + 0.5 * mc[:, 8] * swiglu(modulate(rms_norm(xc), mc[:, 6], mc[:, 7]),
                                              w_ffn_gate[l, 1], w_ffn_up[l, 1], w_ffn_down[l, 1])
    return x
```

```python
import functools
import math

import numpy as np
import jax
import jax.numpy as jnp
from jax import lax
from jax.experimental import pallas as pl
from jax.experimental.pallas import tpu as pltpu

F32 = jnp.float32
BF16 = jnp.bfloat16

LANES = 128
SUBLANES = 8

GRID_W = 64
MLA_HEADS = 8
MLA_NOPE = 64
MLA_ROPE = 32
MLA_QK = MLA_NOPE + MLA_ROPE
MLA_V = 64
MLA_Q_RANK = 768
MLA_KV_RANK = 256
MLA_WIDTH = MLA_HEADS * MLA_V
MLA_SCALE = 1.0 / math.sqrt(MLA_QK)
MLA_PAD = LANES
NA_HEADS = 4
NA_HEAD_DIM = 64
NA_WIDTH = NA_HEADS * NA_HEAD_DIM
NA_WIN_R = 8
NA_WIN_C = 16
NA_SCALE = 1.0 / math.sqrt(NA_HEAD_DIM)
NA_GROUP_ROWS = 4
NA_KEY_ROWS = NA_GROUP_ROWS + NA_WIN_R
POOL_WINDOWS = (2, 4, 8, 16)
POOL_GROUP_DIM = 64
POOL_WIDTH = len(POOL_WINDOWS) * POOL_GROUP_DIM
POOL_HALO = max(POOL_WINDOWS) // 2
N_MOD = 9
ROPE_BASE = 10000.0
EPS = 1e-6
NEG_BIG = -1e30

COL_CQ = 0
COL_CKV = COL_CQ + MLA_Q_RANK
COL_KR = COL_CKV + MLA_KV_RANK
COL_NQ = COL_KR + MLA_PAD
COL_NK = COL_NQ + NA_WIDTH
COL_NV = COL_NK + NA_WIDTH
COL_U = COL_NV + NA_WIDTH
IN_EXT = COL_U + POOL_WIDTH


def _dot(a, b):
    return jnp.dot(a, b, preferred_element_type=F32)


def _dot_t(a, b):
    return lax.dot_general(a, b, (((1,), (1,)), ((), ())), preferred_element_type=F32)


def _silu(x):
    return x * jax.nn.sigmoid(x)


def _cparams(sem, vmem_mb):
    return pltpu.CompilerParams(dimension_semantics=sem, vmem_limit_bytes=vmem_mb * 1024 * 1024)


def _mod_kernel(c_ref, w_ref, b_ref, o_ref):
    h = _silu(c_ref[...]).astype(BF16)
    o_ref[0] = _dot(h, w_ref[0].astype(BF16)) + b_ref[0]


def _modulation(cc, w_mod, b_mod):
    L, D, NM = w_mod.shape
    tn = 1024
    return pl.pallas_call(
        _mod_kernel,
        grid=(L, NM // tn),
        in_specs=[pl.BlockSpec((SUBLANES, D), lambda l, j: (0, 0)),
                  pl.BlockSpec((1, D, tn), lambda l, j: (l, 0, j)),
                  pl.BlockSpec((1, 1, tn), lambda l, j: (l, 0, j))],
        out_specs=pl.BlockSpec((1, SUBLANES, tn), lambda l, j: (l, 0, j)),
        out_shape=jax.ShapeDtypeStruct((L, SUBLANES, NM), F32),
        compiler_params=_cparams(("arbitrary", "arbitrary"), 32),
        name="modulation",
    )(cc, w_mod, b_mod.reshape(L, 1, NM))


def _modulated_norm(x, shift, scale):
    ms = jnp.mean(x * x, axis=-1, keepdims=True)
    return (x * lax.rsqrt(ms + EPS)) * (1.0 + scale) + shift


def _ffn_kernel(x_ref, sh_ref, sc_ref, gt_ref, wg_ref, wu_ref, wd_ref, o_ref, h_scr, acc_scr):
    j = pl.program_id(1)

    @pl.when(j == 0)
    def _():
        h_scr[...] = _modulated_norm(x_ref[...], sh_ref[0], sc_ref[0]).astype(BF16)
        acc_scr[...] = jnp.zeros_like(acc_scr)

    h = h_scr[...]
    g = _dot(h, wg_ref[...])
    u = _dot(h, wu_ref[...])
    acc_scr[...] += _dot((_silu(g) * u).astype(BF16), wd_ref[...])

    @pl.when(j == pl.num_programs(1) - 1)
    def _():
        o_ref[...] = x_ref[...] + (0.5 * gt_ref[0]) * acc_scr[...]


def _ffn(x, shift, scale, gate, wg, wu, wd):
    T, D = x.shape
    F = wg.shape[1]
    nb = shift.shape[0]
    tm = min(512, T // nb)
    tf = F // 2
    per = (T // nb) // tm
    row = lambda i, j: (i // per, 0, 0)
    return pl.pallas_call(
        _ffn_kernel,
        grid=(T // tm, F // tf),
        in_specs=[pl.BlockSpec((tm, D), lambda i, j: (i, 0)),
                  pl.BlockSpec((1, 1, D), row),
                  pl.BlockSpec((1, 1, D), row),
                  pl.BlockSpec((1, 1, D), row),
                  pl.BlockSpec((D, tf), lambda i, j: (0, j)),
                  pl.BlockSpec((D, tf), lambda i, j: (0, j)),
                  pl.BlockSpec((tf, D), lambda i, j: (j, 0))],
        out_specs=pl.BlockSpec((tm, D), lambda i, j: (i, 0)),
        out_shape=jax.ShapeDtypeStruct((T, D), F32),
        scratch_shapes=[pltpu.VMEM((tm, D), BF16), pltpu.VMEM((tm, D), F32)],
        compiler_params=_cparams(("arbitrary", "arbitrary"), 48),
        name="ffn",
    )(x, shift, scale, gate, wg, wu, wd)


def _gain_norm(x, gain, width):
    ms = jnp.sum(x * x, axis=-1, keepdims=True) * (1.0 / width)
    return x * lax.rsqrt(ms + EPS) * gain


def _rope(x, cos, sinp, sinm):
    return x * cos + pltpu.roll(x, 8, 1) * sinp + pltpu.roll(x, LANES - 8, 1) * sinm


def _group_mean_square(x, bd):
    xx = x * x
    hi = xx.astype(BF16)
    lo = (xx - hi.astype(F32)).astype(BF16)
    return _dot(hi, bd) + _dot(lo, bd)


def _pre_kernel(x_ref, sh_ref, sc_ref, win_ref, gcq_ref, wq_ref, gckv_ref, wkv_ref, gq_ref, gk_ref,
                cos_ref, sinp_ref, sinm_ref, bd_ref, gnq_ref, gnk_ref,
                q_ref, k_ref, v_ref, nq_ref, nk_ref, nv_ref, u_ref):
    h = _modulated_norm(x_ref[...], sh_ref[0], sc_ref[0]).astype(BF16)
    z = _dot(h, win_ref[...])
    cos, sinp, sinm = cos_ref[...], sinp_ref[...], sinm_ref[...]

    cq = _gain_norm(z[:, COL_CQ:COL_CKV], gcq_ref[...], MLA_Q_RANK).astype(BF16)
    qf = _dot(cq, wq_ref[...])
    ckv = _gain_norm(z[:, COL_CKV:COL_KR], gckv_ref[...], MLA_KV_RANK).astype(BF16)
    kvf = _dot(ckv, wkv_ref[...])
    kr = z[:, COL_KR:COL_NQ]
    gq, gk = gq_ref[...], gk_ref[...]
    for hd in range(MLA_HEADS):
        sl = slice(hd * MLA_PAD, (hd + 1) * MLA_PAD)
        qh = _rope(_gain_norm(qf[:, sl], gq, MLA_QK), cos, sinp, sinm)
        q_ref[:, sl] = (qh * MLA_SCALE).astype(BF16)
        kh = _rope(_gain_norm(kvf[:, sl] + kr, gk, MLA_QK), cos, sinp, sinm)
        k_ref[:, sl] = kh.astype(BF16)
    v_ref[...] = kvf[:, MLA_HEADS * MLA_PAD:].astype(BF16)

    bd = bd_ref[...]
    nq = z[:, COL_NQ:COL_NK]
    nq_ref[...] = (nq * lax.rsqrt(_group_mean_square(nq, bd) + EPS) * (gnq_ref[...] * NA_SCALE)).astype(BF16)
    nk = z[:, COL_NK:COL_NV]
    nk_ref[...] = (nk * lax.rsqrt(_group_mean_square(nk, bd) + EPS) * gnk_ref[...]).astype(BF16)
    nv_ref[...] = z[:, COL_NV:COL_U].astype(BF16)
    u_ref[...] = z[:, COL_U:IN_EXT]


def _pre(x, shift, scale, lw, tables, seq_len):
    T, D = x.shape
    nb = shift.shape[0]
    tm = min(512, seq_len)
    per = (T // nb) // tm
    per_seq = seq_len // tm
    row = lambda i: (i // per, 0, 0)
    tok = lambda w: pl.BlockSpec((tm, w), lambda i: (i, 0))
    full = lambda a: pl.BlockSpec(a.shape, lambda i: (0,) * a.ndim)
    pos = pl.BlockSpec((tm, LANES), lambda i: (i % per_seq, 0))
    weights = (lw["w_in"], lw["g_cq"], lw["w_q"], lw["g_ckv"], lw["w_kv"], lw["g_q"], lw["g_k"])
    tail = (lw["bd"], lw["g_nq"], lw["g_nk"])
    hq = MLA_HEADS * MLA_PAD
    return pl.pallas_call(
        _pre_kernel,
        grid=(T // tm,),
        in_specs=[tok(D), pl.BlockSpec((1, 1, D), row), pl.BlockSpec((1, 1, D), row)]
                 + [full(a) for a in weights] + [pos, pos, pos] + [full(a) for a in tail],
        out_specs=[tok(hq), tok(hq), tok(MLA_WIDTH), tok(NA_WIDTH), tok(NA_WIDTH), tok(NA_WIDTH), tok(POOL_WIDTH)],
        out_shape=[jax.ShapeDtypeStruct((T, hq), BF16), jax.ShapeDtypeStruct((T, hq), BF16),
                   jax.ShapeDtypeStruct((T, MLA_WIDTH), BF16), jax.ShapeDtypeStruct((T, NA_WIDTH), BF16),
                   jax.ShapeDtypeStruct((T, NA_WIDTH), BF16), jax.ShapeDtypeStruct((T, NA_WIDTH), BF16),
                   jax.ShapeDtypeStruct((T, POOL_WIDTH), F32)],
        compiler_params=_cparams(("arbitrary",), 56),
        name="mix_in",
    )(x, shift, scale, *weights, *tables, *tail)


def _flash_kernel(q_ref, k_ref, v_ref, o_ref, *, tk):
    tq = q_ref.shape[1]
    nk = k_ref.shape[1] // tk
    res = []
    for hd in range(2):
        q = q_ref[0, :, hd * MLA_PAD:(hd + 1) * MLA_PAD]

        def body(j, carry):
            m, l, acc = carry
            start = pl.multiple_of(j * tk, tk)
            ks = k_ref[0, pl.ds(start, tk), hd * MLA_PAD:(hd + 1) * MLA_PAD]
            vs = v_ref[0, pl.ds(start, tk), :]
            s = _dot_t(q, ks)
            m_new = jnp.maximum(m, jnp.max(s, axis=-1, keepdims=True))
            p = jnp.exp(s - m_new)
            alpha = jnp.exp(m - m_new)
            l = alpha * l + jnp.sum(p, axis=-1, keepdims=True)
            acc = alpha * acc + _dot(p.astype(BF16), vs)
            return m_new, l, acc

        init = (jnp.full((tq, 1), NEG_BIG, F32), jnp.zeros((tq, 1), F32), jnp.zeros((tq, 2 * MLA_V), F32))
        m, l, acc = lax.fori_loop(0, nk, body, init)
        res.append(acc / l)
    lane = lax.broadcasted_iota(jnp.int32, (1, 2 * MLA_V), 1)
    o_ref[0] = jnp.where(lane < MLA_V, res[0], res[1]).astype(o_ref.dtype)


def _flash(q, k, v):
    B, Nq, HQ = q.shape
    Nk = k.shape[1]
    pairs = HQ // (2 * MLA_PAD)
    tq = min(1024, Nq)
    tk = next(t for t in (1024, 768, 512, 256) if Nk % t == 0)
    return pl.pallas_call(
        functools.partial(_flash_kernel, tk=tk),
        grid=(B, pairs, Nq // tq),
        in_specs=[pl.BlockSpec((1, tq, 2 * MLA_PAD), lambda b, h, i: (b, i, h)),
                  pl.BlockSpec((1, Nk, 2 * MLA_PAD), lambda b, h, i: (b, 0, h)),
                  pl.BlockSpec((1, Nk, 2 * MLA_V), lambda b, h, i: (b, 0, h))],
        out_specs=pl.BlockSpec((1, tq, 2 * MLA_V), lambda b, h, i: (b, i, h)),
        out_shape=jax.ShapeDtypeStruct((B, Nq, pairs * 2 * MLA_V), BF16),
        compiler_params=_cparams(("arbitrary", "arbitrary", "arbitrary"), 56),
        name="mla_attention",
    )(q, k, v)


def _head_masks(width, head_dim, heads):
    lane = lax.broadcasted_iota(jnp.int32, (1, width), 1)
    return [(lane >= h * head_dim) & (lane < (h + 1) * head_dim) for h in range(heads)]


def _na_kernel(q_ref, k_ref, v_ref, kc_ref, vc_ref, bias_ref, o_ref):
    g = pl.program_id(1)
    rows = k_ref.shape[1] // GRID_W
    base = jnp.clip(g * NA_GROUP_ROWS - NA_WIN_R // 2, 0, rows - NA_KEY_ROWS)
    start = pl.multiple_of(base * GRID_W, GRID_W)
    kw = k_ref[0, pl.ds(start, NA_KEY_ROWS * GRID_W), :]
    vw = v_ref[0, pl.ds(start, NA_KEY_ROWS * GRID_W), :]
    kc, vc = kc_ref[0], vc_ref[0]
    q = q_ref[0]
    out = jnp.zeros(q.shape, F32)
    for hd, mask in enumerate(_head_masks(NA_WIDTH, NA_HEAD_DIM, NA_HEADS)):
        qh = jnp.where(mask, q, jnp.zeros_like(q))
        s_loc = _dot_t(qh, kw) + bias_ref[0, hd]
        s_ctx = _dot_t(qh, kc)
        m = jnp.maximum(jnp.max(s_loc, axis=-1, keepdims=True), jnp.max(s_ctx, axis=-1, keepdims=True))
        p_loc = jnp.exp(s_loc - m)
        p_ctx = jnp.exp(s_ctx - m)
        l = jnp.sum(p_loc, axis=-1, keepdims=True) + jnp.sum(p_ctx, axis=-1, keepdims=True)
        o = _dot(p_loc.astype(BF16), vw) + _dot(p_ctx.astype(BF16), vc)
        out = jnp.where(mask, o / l, out)
    o_ref[0] = out.astype(o_ref.dtype)


def _na(q, k, v, kc, vc, bias):
    B, S, W = q.shape
    C = kc.shape[1]
    tq = NA_GROUP_ROWS * GRID_W
    groups = S // tq

    def variant(b, g):
        return (jnp.where(g == 0, 0, jnp.where(g == groups - 1, 2, 1)), 0, 0, 0)

    return pl.pallas_call(
        _na_kernel,
        grid=(B, groups),
        in_specs=[pl.BlockSpec((1, tq, W), lambda b, g: (b, g, 0)),
                  pl.BlockSpec((1, S, W), lambda b, g: (b, 0, 0)),
                  pl.BlockSpec((1, S, W), lambda b, g: (b, 0, 0)),
                  pl.BlockSpec((1, C, W), lambda b, g: (b, 0, 0)),
                  pl.BlockSpec((1, C, W), lambda b, g: (b, 0, 0)),
                  pl.BlockSpec((1,) + bias.shape[1:], variant)],
        out_specs=pl.BlockSpec((1, tq, W), lambda b, g: (b, g, 0)),
        out_shape=jax.ShapeDtypeStruct((B, S, W), BF16),
        compiler_params=_cparams(("arbitrary", "arbitrary"), 56),
        name="na_attention",
    )(q, k, v, kc, vc, bias)


def _na_bias_tables(rpb):
    i = np.arange(NA_GROUP_ROWS)[:, None, None, None]
    w = np.arange(GRID_W)[None, :, None, None]
    jj = np.arange(NA_KEY_ROWS)[None, None, :, None]
    kcol = np.arange(GRID_W)[None, None, None, :]
    half = NA_WIN_R // 2
    q_rel = (i, half + i, NA_WIN_R + i)
    r0_rel = (0 * i, i, half + 0 * i)
    cs = np.clip(w - NA_WIN_C // 2, 0, GRID_W - NA_WIN_C)
    col_ok = (kcol >= cs) & (kcol < cs + NA_WIN_C)
    col_off = np.clip(kcol - w + NA_WIN_C - 1, 0, 2 * NA_WIN_C - 2)
    tabs = []
    shape = (NA_GROUP_ROWS, GRID_W, NA_KEY_ROWS, GRID_W)
    flat = (NA_GROUP_ROWS * GRID_W, NA_KEY_ROWS * GRID_W)
    for qr, r0 in zip(q_rel, r0_rel):
        row_ok = (jj >= r0) & (jj < r0 + NA_WIN_R)
        row_off = np.clip(jj - qr + NA_WIN_R - 1, 0, 2 * NA_WIN_R - 2)
        ok = np.broadcast_to(row_ok & col_ok, shape).reshape(flat)
        ro = np.broadcast_to(row_off, shape).reshape(flat)
        co = np.broadcast_to(col_off, shape).reshape(flat)
        tabs.append(jnp.where(ok[None], rpb[:, ro, co], NEG_BIG))
    return jnp.stack(tabs).astype(F32)


def _small_attn_kernel(q_ref, k_ref, v_ref, o_ref):
    q, k, v = q_ref[0], k_ref[0], v_ref[0]
    out = jnp.zeros(q.shape, F32)
    for mask in _head_masks(NA_WIDTH, NA_HEAD_DIM, NA_HEADS):
        s = _dot_t(jnp.where(mask, q, jnp.zeros_like(q)), k)
        p = jnp.exp(s - jnp.max(s, axis=-1, keepdims=True))
        o = _dot(p.astype(BF16), v)
        out = jnp.where(mask, o / jnp.sum(p, axis=-1, keepdims=True), out)
    o_ref[0] = out.astype(o_ref.dtype)


def _small_attn(q, k, v):
    B, N, W = q.shape
    spec = pl.BlockSpec((1, N, W), lambda b: (b, 0, 0))
    return pl.pallas_call(
        _small_attn_kernel, grid=(B,), in_specs=[spec, spec, spec], out_specs=spec,
        out_shape=jax.ShapeDtypeStruct((B, N, W), BF16),
        compiler_params=_cparams(("arbitrary",), 32),
        name="ctx_na_attention",
    )(q, k, v)


def _mix_out_kernel(x_ref, gt_ref, ya_ref, yb_ref, u_ref, up_ref, un_ref, wo_ref, wp_ref, ps_ref, o_ref,
                    *, per_seq, seq_len):
    tm = x_ref.shape[0]
    i = pl.program_id(0) % per_seq
    u = u_ref[...]
    prev = jnp.where(i == 0, 0.0, up_ref[...])
    nxt = jnp.where(i == per_seq - 1, 0.0, un_ref[...])
    ext = jnp.concatenate([prev, u, nxt], axis=0)
    n = tm + 2 * POOL_HALO
    sums = []
    p = ext + pltpu.roll(ext, 1, 0)
    sums.append(p)
    for step in (1, 2, 4):
        p = pltpu.roll(p, step, 0) + pltpu.roll(p, n - step, 0)
        sums.append(p)
    sums = [s[POOL_HALO:POOL_HALO + tm] for s in sums]
    lane = lax.broadcasted_iota(jnp.int32, (1, POOL_WIDTH), 1)
    grp = lane // POOL_GROUP_DIM
    win = jnp.where(grp == 0, sums[0], jnp.where(grp == 1, sums[1], jnp.where(grp == 2, sums[2], sums[3])))
    half = jnp.where(grp == 0, POOL_WINDOWS[0] // 2,
                     jnp.where(grp == 1, POOL_WINDOWS[1] // 2,
                               jnp.where(grp == 2, POOL_WINDOWS[2] // 2, POOL_WINDOWS[3] // 2)))
    t = i * tm + lax.broadcasted_iota(jnp.int32, (tm, 1), 0)
    cnt = jnp.minimum(t + half, seq_len) - jnp.maximum(t - half, 0)
    pooled = win / cnt.astype(F32) - u
    yc = _dot(pooled.astype(BF16), wp_ref[...]) * ps_ref[...]
    wa = wo_ref[0:MLA_WIDTH, :]
    wb = wo_ref[MLA_WIDTH:MLA_WIDTH + NA_WIDTH, :]
    wc = wo_ref[MLA_WIDTH + NA_WIDTH:, :]
    y = _dot(ya_ref[...], wa) + _dot(yb_ref[...], wb) + _dot(yc.astype(BF16), wc)
    o_ref[...] = x_ref[...] + gt_ref[0] * y


def _mix_out(x, gate, ya, yb, u, lw, seq_len):
    T, D = x.shape
    nb = gate.shape[0]
    tm = min(512, seq_len)
    per = (T // nb) // tm
    per_seq = seq_len // tm
    hb = tm // POOL_HALO
    last_hb = T // POOL_HALO - 1
    tok = lambda w: pl.BlockSpec((tm, w), lambda i: (i, 0))
    full = lambda a: pl.BlockSpec(a.shape, lambda i: (0,) * a.ndim)
    return pl.pallas_call(
        functools.partial(_mix_out_kernel, per_seq=per_seq, seq_len=seq_len),
        grid=(T // tm,),
        in_specs=[tok(D), pl.BlockSpec((1, 1, D), lambda i: (i // per, 0, 0)),
                  tok(MLA_WIDTH), tok(NA_WIDTH), tok(POOL_WIDTH),
                  pl.BlockSpec((POOL_HALO, POOL_WIDTH), lambda i: (jnp.maximum(i * hb - 1, 0), 0)),
                  pl.BlockSpec((POOL_HALO, POOL_WIDTH), lambda i: (jnp.minimum((i + 1) * hb, last_hb), 0)),
                  full(lw["w_out"]), full(lw["w_pool"]), full(lw["pool_scale"])],
        out_specs=tok(D),
        out_shape=jax.ShapeDtypeStruct((T, D), F32),
        compiler_params=_cparams(("arbitrary",), 48),
        name="mix_out",
    )(x, gate, ya, yb, u, u, u, lw["w_out"], lw["w_pool"], lw["pool_scale"])


def _rope_tables(n_tok):
    t = np.arange(n_tok)
    axis_dim = MLA_ROPE // 2
    nf = axis_dim // 2
    inv_freq = jnp.asarray(ROPE_BASE, F32) ** (-jnp.arange(0, axis_dim, 2, dtype=F32) / axis_dim)
    ar = jnp.asarray(t // GRID_W, F32)[:, None] * inv_freq
    ac = jnp.asarray(t % GRID_W, F32)[:, None] * inv_freq
    ones = jnp.ones((n_tok, MLA_NOPE), F32)
    zeros = jnp.zeros((n_tok, MLA_NOPE), F32)
    zf = jnp.zeros((n_tok, nf), F32)
    pad1 = jnp.ones((n_tok, MLA_PAD - MLA_QK), F32)
    pad0 = jnp.zeros((n_tok, MLA_PAD - MLA_QK), F32)
    cos = jnp.concatenate([ones, jnp.cos(ar), jnp.cos(ar), jnp.cos(ac), jnp.cos(ac), pad1], axis=1)
    sinp = jnp.concatenate([zeros, zf, jnp.sin(ar), zf, jnp.sin(ac), pad0], axis=1)
    sinm = jnp.concatenate([zeros, -jnp.sin(ar), zf, -jnp.sin(ac), zf, pad0], axis=1)
    return cos, sinp, sinm


def _identity_tables(n_tok):
    return (jnp.ones((n_tok, MLA_PAD), F32), jnp.zeros((n_tok, MLA_PAD), F32), jnp.zeros((n_tok, MLA_PAD), F32))


def _layer_weights(l, w_in, g_cq, w_q_up, g_ckv, w_kv_up, g_mla_q, g_mla_k, g_na_q, g_na_k,
                   w_pool, pool_scale, w_out):
    D = w_in.shape[1]
    wi = w_in[l]
    kr_cols = jnp.zeros((D, MLA_PAD), F32).at[:, MLA_NOPE:MLA_QK].set(wi[:, COL_KR:COL_KR + MLA_ROPE])
    w_in_ext = jnp.concatenate([wi[:, :COL_KR], kr_cols, wi[:, COL_KR + MLA_ROPE:]], axis=1).astype(BF16)
    wq = w_q_up[l].reshape(MLA_Q_RANK, MLA_HEADS, MLA_QK)
    wq = jnp.pad(wq, ((0, 0), (0, 0), (0, MLA_PAD - MLA_QK))).reshape(MLA_Q_RANK, MLA_HEADS * MLA_PAD)
    wkv = w_kv_up[l].reshape(MLA_KV_RANK, MLA_HEADS, MLA_NOPE + MLA_V)
    wk = jnp.pad(wkv[:, :, :MLA_NOPE], ((0, 0), (0, 0), (0, MLA_PAD - MLA_NOPE)))
    wv = wkv[:, :, MLA_NOPE:]
    w_kv = jnp.concatenate([wk.reshape(MLA_KV_RANK, -1), wv.reshape(MLA_KV_RANK, -1)], axis=1)
    pad_gain = lambda g: jnp.pad(g, (0, MLA_PAD - MLA_QK)).reshape(1, MLA_PAD)
    head_id = np.arange(NA_WIDTH) // NA_HEAD_DIM
    bd = jnp.asarray((head_id[:, None] == head_id[None, :]) / NA_HEAD_DIM, BF16)
    wp = jnp.zeros((POOL_WIDTH, POOL_WIDTH), F32)
    for g in range(len(POOL_WINDOWS)):
        sl = slice(g * POOL_GROUP_DIM, (g + 1) * POOL_GROUP_DIM)
        wp = wp.at[sl, sl].set(w_pool[l, g])
    return {
        "w_in": w_in_ext, "g_cq": g_cq[l].reshape(1, -1), "w_q": wq.astype(BF16),
        "g_ckv": g_ckv[l].reshape(1, -1), "w_kv": w_kv.astype(BF16),
        "g_q": pad_gain(g_mla_q[l]), "g_k": pad_gain(g_mla_k[l]),
        "bd": bd, "g_nq": jnp.tile(g_na_q[l], NA_HEADS).reshape(1, -1), "g_nk": jnp.tile(g_na_k[l], NA_HEADS).reshape(1, -1),
        "w_pool": wp.astype(BF16), "pool_scale": pool_scale[l].reshape(1, -1), "w_out": w_out[l].astype(BF16),
    }


def kernel(x, c, ctx, c_ctx, w_mod, b_mod, w_ffn_gate, w_ffn_up, w_ffn_down, w_in, g_cq, w_q_up, g_ckv, w_kv_up,
           g_mla_q, g_mla_k, g_na_q, g_na_k, na_rpb, w_pool, pool_scale, w_out):
    B, S, D = x.shape
    C = ctx.shape[1]
    L = w_mod.shape[0]
    assert S % (NA_GROUP_ROWS * GRID_W) == 0 and S // GRID_W >= NA_KEY_ROWS and B + 1 <= SUBLANES

    cc = jnp.zeros((SUBLANES, D), F32).at[:B].set(c).at[B].set(c_ctx)
    mods = _modulation(cc, w_mod, b_mod).reshape(L, SUBLANES, N_MOD, 1, D)
    rope = _rope_tables(S)
    no_rope = _identity_tables(C)
    wg, wu, wd = w_ffn_gate.astype(BF16), w_ffn_up.astype(BF16), w_ffn_down.astype(BF16)

    xs = x.reshape(B * S, D)
    xc = ctx.reshape(B * C, D)
    for l in range(L):
        last = l == L - 1
        mx = [mods[l, :B, j] for j in range(N_MOD)]
        mc = [mods[l, B:B + 1, j] for j in range(N_MOD)]
        lw = _layer_weights(l, w_in, g_cq, w_q_up, g_ckv, w_kv_up, g_mla_q, g_mla_k, g_na_q, g_na_k,
                            w_pool, pool_scale, w_out)
        xs = _ffn(xs, mx[0], mx[1], mx[2], wg[l, 0], wu[l, 0], wd[l, 0])
        xc = _ffn(xc, mc[0], mc[1], mc[2], wg[l, 0], wu[l, 0], wd[l, 0])

        qx, kx, vx, nqx, nkx, nvx, ux = _pre(xs, mx[3], mx[4], lw, rope, S)
        qc, kc, vc, nqc, nkc, nvc, uc = _pre(xc, mc[3], mc[4], lw, no_rope, C)
        per_b = lambda a, n: a.reshape(B, n, a.shape[-1])
        k_all = jnp.concatenate([per_b(kc, C), per_b(kx, S)], axis=1)
        v_all = jnp.concatenate([per_b(vc, C), per_b(vx, S)], axis=1)
        ya = _flash(per_b(qx, S), k_all, v_all)
        yb = _na(per_b(nqx, S), per_b(nkx, S), per_b(nvx, S), per_b(nkc, C), per_b(nvc, C),
                 _na_bias_tables(na_rpb[l]))
        xs = _mix_out(xs, mx[5], ya.reshape(B * S, -1), yb.reshape(B * S, -1), ux, lw, S)
        xs = _ffn(xs, mx[6], mx[7], mx[8], wg[l, 1], wu[l, 1], wd[l, 1])
        if not last:
            ya_c = _flash(per_b(qc, C), per_b(kc, C), per_b(vc, C))
            yb_c = _small_attn(per_b(nqc, C), per_b(nkc, C), per_b(nvc, C))
            xc = _mix_out(xc, mc[5], ya_c.reshape(B * C, -1), yb_c.reshape(B * C, -1), uc, lw, C)
            xc = _ffn(xc, mc[6], mc[7], mc[8], wg[l, 1], wu[l, 1], wd[l, 1])
    return xs.reshape(B, S, D)
```

```python
import functools
import math

import numpy as np
import jax
import jax.numpy as jnp
from jax import lax
from jax.experimental import pallas as pl
from jax.experimental.pallas import tpu as pltpu

F32 = jnp.float32
BF16 = jnp.bfloat16

LANES = 128
SUBLANES = 8

GRID_W = 64
MLA_HEADS = 8
MLA_NOPE = 64
MLA_ROPE = 32
MLA_QK = MLA_NOPE + MLA_ROPE
MLA_V = 64
MLA_Q_RANK = 768
MLA_KV_RANK = 256
MLA_WIDTH = MLA_HEADS * MLA_V
MLA_SCALE = 1.0 / math.sqrt(MLA_QK)
MLA_PAD = LANES
MLA_SCALE_LOG2 = MLA_SCALE * math.log2(math.e)
MLA_SHIFT_LANE = MLA_QK
MLA_FAST_BOUND = 60.0
MLA_KEY_TILE = 256
NA_HEADS = 4
NA_HEAD_DIM = 64
NA_WIDTH = NA_HEADS * NA_HEAD_DIM
NA_WIN_R = 8
NA_WIN_C = 16
NA_SCALE = 1.0 / math.sqrt(NA_HEAD_DIM)
NA_GROUP_ROWS = 4
NA_KEY_ROWS = NA_GROUP_ROWS + NA_WIN_R
POOL_WINDOWS = (2, 4, 8, 16)
POOL_GROUP_DIM = 64
POOL_WIDTH = len(POOL_WINDOWS) * POOL_GROUP_DIM
POOL_HALO = max(POOL_WINDOWS) // 2
N_MOD = 9
ROPE_BASE = 10000.0
EPS = 1e-6
NEG_BIG = -1e30

COL_CQ = 0
COL_CKV = COL_CQ + MLA_Q_RANK
COL_KR = COL_CKV + MLA_KV_RANK
COL_NQ = COL_KR + MLA_PAD
COL_NK = COL_NQ + NA_WIDTH
COL_NV = COL_NK + NA_WIDTH
COL_U = COL_NV + NA_WIDTH
IN_EXT = COL_U + POOL_WIDTH


def _dot(a, b):
    return jnp.dot(a, b, preferred_element_type=F32)


def _dot_t(a, b):
    return lax.dot_general(a, b, (((1,), (1,)), ((), ())), preferred_element_type=F32)


def _silu(x):
    return x * jax.nn.sigmoid(x)


def _cparams(sem, vmem_mb):
    return pltpu.CompilerParams(dimension_semantics=sem, vmem_limit_bytes=vmem_mb * 1024 * 1024)


def _mod_kernel(c_ref, w_ref, b_ref, o_ref):
    h = _silu(c_ref[...]).astype(BF16)
    o_ref[0] = _dot(h, w_ref[0].astype(BF16)) + b_ref[0]


def _modulation(cc, w_mod, b_mod):
    L, D, NM = w_mod.shape
    tn = 1024
    return pl.pallas_call(
        _mod_kernel,
        grid=(L, NM // tn),
        in_specs=[pl.BlockSpec((SUBLANES, D), lambda l, j: (0, 0)),
                  pl.BlockSpec((1, D, tn), lambda l, j: (l, 0, j)),
                  pl.BlockSpec((1, 1, tn), lambda l, j: (l, 0, j))],
        out_specs=pl.BlockSpec((1, SUBLANES, tn), lambda l, j: (l, 0, j)),
        out_shape=jax.ShapeDtypeStruct((L, SUBLANES, NM), F32),
        compiler_params=_cparams(("arbitrary", "arbitrary"), 32),
        name="modulation",
    )(cc, w_mod, b_mod.reshape(L, 1, NM))


def _modulated_norm(x, shift, scale):
    ms = jnp.mean(x * x, axis=-1, keepdims=True)
    return (x * lax.rsqrt(ms + EPS)) * (1.0 + scale) + shift


def _ffn_kernel(x_ref, sh_ref, sc_ref, gt_ref, wg_ref, wu_ref, wd_ref, o_ref, h_scr, acc_scr):
    j = pl.program_id(1)

    @pl.when(j == 0)
    def _():
        h_scr[...] = _modulated_norm(x_ref[...], sh_ref[0], sc_ref[0]).astype(BF16)
        acc_scr[...] = jnp.zeros_like(acc_scr)

    h = h_scr[...]
    g = _dot(h, wg_ref[...])
    u = _dot(h, wu_ref[...])
    acc_scr[...] += _dot((_silu(g) * u).astype(BF16), wd_ref[...])

    @pl.when(j == pl.num_programs(1) - 1)
    def _():
        o_ref[...] = x_ref[...] + (0.5 * gt_ref[0]) * acc_scr[...]


def _ffn(x, shift, scale, gate, wg, wu, wd):
    T, D = x.shape
    F = wg.shape[1]
    nb = shift.shape[0]
    tm = min(512, T // nb)
    tf = F // 2
    per = (T // nb) // tm
    row = lambda i, j: (i // per, 0, 0)
    return pl.pallas_call(
        _ffn_kernel,
        grid=(T // tm, F // tf),
        in_specs=[pl.BlockSpec((tm, D), lambda i, j: (i, 0)),
                  pl.BlockSpec((1, 1, D), row),
                  pl.BlockSpec((1, 1, D), row),
                  pl.BlockSpec((1, 1, D), row),
                  pl.BlockSpec((D, tf), lambda i, j: (0, j)),
                  pl.BlockSpec((D, tf), lambda i, j: (0, j)),
                  pl.BlockSpec((tf, D), lambda i, j: (j, 0))],
        out_specs=pl.BlockSpec((tm, D), lambda i, j: (i, 0)),
        out_shape=jax.ShapeDtypeStruct((T, D), F32),
        scratch_shapes=[pltpu.VMEM((tm, D), BF16), pltpu.VMEM((tm, D), F32)],
        compiler_params=_cparams(("arbitrary", "arbitrary"), 48),
        name="ffn",
    )(x, shift, scale, gate, wg, wu, wd)


def _gain_norm(x, gain, width):
    ms = jnp.sum(x * x, axis=-1, keepdims=True) * (1.0 / width)
    return x * lax.rsqrt(ms + EPS) * gain


def _rope(x, cos, sinp, sinm):
    return x * cos + pltpu.roll(x, 8, 1) * sinp + pltpu.roll(x, LANES - 8, 1) * sinm


def _mla_score_bound(gq, gk):
    amax = lambda g: jnp.max(jnp.abs(g), axis=-1, keepdims=True)
    return (MLA_SCALE_LOG2 * MLA_QK) * amax(gq) * amax(gk)


def _group_mean_square(x, bd):
    xx = x * x
    hi = xx.astype(BF16)
    lo = (xx - hi.astype(F32)).astype(BF16)
    return _dot(hi, bd) + _dot(lo, bd)


def _pre_kernel(x_ref, sh_ref, sc_ref, win_ref, gcq_ref, wq_ref, gckv_ref, wkv_ref, gq_ref, gk_ref,
                cos_ref, sinp_ref, sinm_ref, bd_ref, gnq_ref, gnk_ref,
                q_ref, k_ref, v_ref, nq_ref, nk_ref, nv_ref, u_ref):
    h = _modulated_norm(x_ref[...], sh_ref[0], sc_ref[0]).astype(BF16)
    z = _dot(h, win_ref[...])
    cos, sinp, sinm = cos_ref[...], sinp_ref[...], sinm_ref[...]

    cq = _gain_norm(z[:, COL_CQ:COL_CKV], gcq_ref[...], MLA_Q_RANK).astype(BF16)
    qf = _dot(cq, wq_ref[...])
    ckv = _gain_norm(z[:, COL_CKV:COL_KR], gckv_ref[...], MLA_KV_RANK).astype(BF16)
    kvf = _dot(ckv, wkv_ref[...])
    kr = z[:, COL_KR:COL_NQ]
    gq, gk = gq_ref[...], gk_ref[...]
    shift_lane = lax.broadcasted_iota(jnp.int32, (1, MLA_PAD), 1) == MLA_SHIFT_LANE
    neg_bound = -_mla_score_bound(gq, gk)
    for hd in range(MLA_HEADS):
        sl = slice(hd * MLA_PAD, (hd + 1) * MLA_PAD)
        qh = _rope(_gain_norm(qf[:, sl], gq, MLA_QK), cos, sinp, sinm) * MLA_SCALE_LOG2
        q_ref[:, sl] = jnp.where(shift_lane, neg_bound, qh).astype(BF16)
        kh = _rope(_gain_norm(kvf[:, sl] + kr, gk, MLA_QK), cos, sinp, sinm)
        k_ref[:, sl] = jnp.where(shift_lane, 1.0, kh).astype(BF16)
    ones = jnp.ones((v_ref.shape[0], MLA_PAD), BF16)
    for pr in range(MLA_HEADS // 2):
        vcols = slice(MLA_HEADS * MLA_PAD + pr * MLA_PAD, MLA_HEADS * MLA_PAD + (pr + 1) * MLA_PAD)
        v_ref[:, 2 * pr * MLA_PAD:(2 * pr + 1) * MLA_PAD] = kvf[:, vcols].astype(BF16)
        v_ref[:, (2 * pr + 1) * MLA_PAD:(2 * pr + 2) * MLA_PAD] = ones

    bd = bd_ref[...]
    nq = z[:, COL_NQ:COL_NK]
    nq_ref[...] = (nq * lax.rsqrt(_group_mean_square(nq, bd) + EPS) * (gnq_ref[...] * NA_SCALE)).astype(BF16)
    nk = z[:, COL_NK:COL_NV]
    nk_ref[...] = (nk * lax.rsqrt(_group_mean_square(nk, bd) + EPS) * gnk_ref[...]).astype(BF16)
    nv_ref[...] = z[:, COL_NV:COL_U].astype(BF16)
    u_ref[...] = z[:, COL_U:IN_EXT]


def _pre(x, shift, scale, lw, tables, seq_len):
    T, D = x.shape
    nb = shift.shape[0]
    tm = min(512, seq_len)
    per = (T // nb) // tm
    per_seq = seq_len // tm
    row = lambda i: (i // per, 0, 0)
    tok = lambda w: pl.BlockSpec((tm, w), lambda i: (i, 0))
    full = lambda a: pl.BlockSpec(a.shape, lambda i: (0,) * a.ndim)
    pos = pl.BlockSpec((tm, LANES), lambda i: (i % per_seq, 0))
    weights = (lw["w_in"], lw["g_cq"], lw["w_q"], lw["g_ckv"], lw["w_kv"], lw["g_q"], lw["g_k"])
    tail = (lw["bd"], lw["g_nq"], lw["g_nk"])
    hq = MLA_HEADS * MLA_PAD
    return pl.pallas_call(
        _pre_kernel,
        grid=(T // tm,),
        in_specs=[tok(D), pl.BlockSpec((1, 1, D), row), pl.BlockSpec((1, 1, D), row)]
                 + [full(a) for a in weights] + [pos, pos, pos] + [full(a) for a in tail],
        out_specs=[tok(hq), tok(hq), tok(hq), tok(NA_WIDTH), tok(NA_WIDTH), tok(NA_WIDTH), tok(POOL_WIDTH)],
        out_shape=[jax.ShapeDtypeStruct((T, hq), BF16), jax.ShapeDtypeStruct((T, hq), BF16),
                   jax.ShapeDtypeStruct((T, hq), BF16), jax.ShapeDtypeStruct((T, NA_WIDTH), BF16),
                   jax.ShapeDtypeStruct((T, NA_WIDTH), BF16), jax.ShapeDtypeStruct((T, NA_WIDTH), BF16),
                   jax.ShapeDtypeStruct((T, POOL_WIDTH), F32)],
        compiler_params=_cparams(("arbitrary",), 56),
        name="mix_in",
    )(x, shift, scale, *weights, *tables, *tail)


def _flash_fast_kernel(q_ref, *refs):
    o_ref = refs[-1]
    tq = q_ref.shape[1]
    for hd in range(2):
        lanes = slice(hd * MLA_PAD, (hd + 1) * MLA_PAD)
        q = q_ref[0, :, lanes]
        acc = jnp.zeros((tq, 2 * MLA_PAD), F32)
        for k_ref, v_ref in zip(refs[0:-1:2], refs[1:-1:2]):
            for t in range(k_ref.shape[1] // MLA_KEY_TILE):
                keys = slice(t * MLA_KEY_TILE, (t + 1) * MLA_KEY_TILE)
                p = jnp.exp2(_dot_t(q, k_ref[0, keys, lanes])).astype(BF16)
                acc = acc + _dot(p, v_ref[0, keys, :])
        own = slice(hd * MLA_V, (hd + 1) * MLA_V)
        o_ref[0, :, own] = (acc[:, own] / acc[:, MLA_PAD:MLA_PAD + 1]).astype(o_ref.dtype)


def _flash_fast(q, kv):
    B, Nq, HQ = q.shape
    pairs = HQ // (2 * MLA_PAD)
    tq = min(1024, Nq)
    kv_specs = [pl.BlockSpec((1, a.shape[1], 2 * MLA_PAD), lambda b, h, i: (b, 0, h)) for pair in kv for a in pair]
    return pl.pallas_call(
        _flash_fast_kernel,
        grid=(B, pairs, Nq // tq),
        in_specs=[pl.BlockSpec((1, tq, 2 * MLA_PAD), lambda b, h, i: (b, i, h))] + kv_specs,
        out_specs=pl.BlockSpec((1, tq, 2 * MLA_V), lambda b, h, i: (b, i, h)),
        out_shape=jax.ShapeDtypeStruct((B, Nq, pairs * 2 * MLA_V), BF16),
        compiler_params=_cparams(("arbitrary", "arbitrary", "arbitrary"), 56),
        name="mla_attention",
    )(q, *[a for pair in kv for a in pair])


def _flash_safe_kernel(q_ref, k_ref, v_ref, o_ref, *, tk):
    tq = q_ref.shape[1]
    nk = k_ref.shape[1] // tk
    for hd in range(2):
        lanes = slice(hd * MLA_PAD, (hd + 1) * MLA_PAD)
        q = q_ref[0, :, lanes]

        def body(j, carry):
            m, acc = carry
            start = pl.multiple_of(j * tk, tk)
            s = _dot_t(q, k_ref[0, pl.ds(start, tk), lanes])
            m_new = jnp.maximum(m, jnp.max(s, axis=-1, keepdims=True))
            p = jnp.exp2(s - m_new)
            acc = jnp.exp2(m - m_new) * acc + _dot(p.astype(BF16), v_ref[0, pl.ds(start, tk), :])
            return m_new, acc

        init = (jnp.full((tq, 1), NEG_BIG, F32), jnp.zeros((tq, 2 * MLA_PAD), F32))
        _, acc = lax.fori_loop(0, nk, body, init)
        own = slice(hd * MLA_V, (hd + 1) * MLA_V)
        o_ref[0, :, own] = (acc[:, own] / acc[:, MLA_PAD:MLA_PAD + 1]).astype(o_ref.dtype)


def _flash_safe(q, kv):
    k = jnp.concatenate([pair[0] for pair in kv], axis=1)
    v = jnp.concatenate([pair[1] for pair in kv], axis=1)
    B, Nq, HQ = q.shape
    Nk = k.shape[1]
    pairs = HQ // (2 * MLA_PAD)
    tq = min(512, Nq)
    tk = next(t for t in (768, 512, 256) if Nk % t == 0)
    kv_spec = pl.BlockSpec((1, Nk, 2 * MLA_PAD), lambda b, h, i: (b, 0, h))
    return pl.pallas_call(
        functools.partial(_flash_safe_kernel, tk=tk),
        grid=(B, pairs, Nq // tq),
        in_specs=[pl.BlockSpec((1, tq, 2 * MLA_PAD), lambda b, h, i: (b, i, h)), kv_spec, kv_spec],
        out_specs=pl.BlockSpec((1, tq, 2 * MLA_V), lambda b, h, i: (b, i, h)),
        out_shape=jax.ShapeDtypeStruct((B, Nq, pairs * 2 * MLA_V), BF16),
        compiler_params=_cparams(("arbitrary", "arbitrary", "arbitrary"), 56),
        name="mla_attention_running_max",
    )(q, k, v)


def _mla_attention(q, kv, bound):
    return lax.cond(bound <= MLA_FAST_BOUND, lambda: _flash_fast(q, kv), lambda: _flash_safe(q, kv))


def _head_masks(width, head_dim, heads):
    lane = lax.broadcasted_iota(jnp.int32, (1, width), 1)
    return [(lane >= h * head_dim) & (lane < (h + 1) * head_dim) for h in range(heads)]


def _na_kernel(q_ref, k_ref, v_ref, kc_ref, vc_ref, bias_ref, o_ref):
    g = pl.program_id(1)
    rows = k_ref.shape[1] // GRID_W
    base = jnp.clip(g * NA_GROUP_ROWS - NA_WIN_R // 2, 0, rows - NA_KEY_ROWS)
    start = pl.multiple_of(base * GRID_W, GRID_W)
    kw = k_ref[0, pl.ds(start, NA_KEY_ROWS * GRID_W), :]
    vw = v_ref[0, pl.ds(start, NA_KEY_ROWS * GRID_W), :]
    kc, vc = kc_ref[0], vc_ref[0]
    q = q_ref[0]
    out = jnp.zeros(q.shape, F32)
    for hd, mask in enumerate(_head_masks(NA_WIDTH, NA_HEAD_DIM, NA_HEADS)):
        qh = jnp.where(mask, q, jnp.zeros_like(q))
        s_loc = _dot_t(qh, kw) + bias_ref[0, hd]
        s_ctx = _dot_t(qh, kc)
        m = jnp.maximum(jnp.max(s_loc, axis=-1, keepdims=True), jnp.max(s_ctx, axis=-1, keepdims=True))
        p_loc = jnp.exp(s_loc - m)
        p_ctx = jnp.exp(s_ctx - m)
        l = jnp.sum(p_loc, axis=-1, keepdims=True) + jnp.sum(p_ctx, axis=-1, keepdims=True)
        o = _dot(p_loc.astype(BF16), vw) + _dot(p_ctx.astype(BF16), vc)
        out = jnp.where(mask, o / l, out)
    o_ref[0] = out.astype(o_ref.dtype)


def _na(q, k, v, kc, vc, bias):
    B, S, W = q.shape
    C = kc.shape[1]
    tq = NA_GROUP_ROWS * GRID_W
    groups = S // tq

    def variant(b, g):
        return (jnp.where(g == 0, 0, jnp.where(g == groups - 1, 2, 1)), 0, 0, 0)

    return pl.pallas_call(
        _na_kernel,
        grid=(B, groups),
        in_specs=[pl.BlockSpec((1, tq, W), lambda b, g: (b, g, 0)),
                  pl.BlockSpec((1, S, W), lambda b, g: (b, 0, 0)),
                  pl.BlockSpec((1, S, W), lambda b, g: (b, 0, 0)),
                  pl.BlockSpec((1, C, W), lambda b, g: (b, 0, 0)),
                  pl.BlockSpec((1, C, W), lambda b, g: (b, 0, 0)),
                  pl.BlockSpec((1,) + bias.shape[1:], variant)],
        out_specs=pl.BlockSpec((1, tq, W), lambda b, g: (b, g, 0)),
        out_shape=jax.ShapeDtypeStruct((B, S, W), BF16),
        compiler_params=_cparams(("arbitrary", "arbitrary"), 56),
        name="na_attention",
    )(q, k, v, kc, vc, bias)


def _na_bias_tables(rpb):
    i = np.arange(NA_GROUP_ROWS)[:, None]
    jj = np.arange(NA_KEY_ROWS)[None, :]
    half = NA_WIN_R // 2
    q_rel = (i, half + i, NA_WIN_R + i)
    r0_rel = (0 * i, i, half + 0 * i)
    n_r, n_c = 2 * NA_WIN_R - 1, 2 * NA_WIN_C - 1
    row_sel = np.zeros((3, NA_GROUP_ROWS, NA_KEY_ROWS, n_r), np.float32)
    for v, (qr, r0) in enumerate(zip(q_rel, r0_rel)):
        row_ok = (jj >= r0) & (jj < r0 + NA_WIN_R)
        row_off = jj - qr + NA_WIN_R - 1
        row_sel[v] = row_ok[..., None] & (row_off[..., None] == np.arange(n_r))
    w = np.arange(GRID_W)[:, None]
    kcol = np.arange(GRID_W)[None, :]
    cs = np.clip(w - NA_WIN_C // 2, 0, GRID_W - NA_WIN_C)
    col_ok = (kcol >= cs) & (kcol < cs + NA_WIN_C)
    col_sel = (col_ok[..., None] & ((kcol - w + NA_WIN_C - 1)[..., None] == np.arange(n_c))).astype(np.float32)
    tab = jnp.einsum("vijr,hrc,wkc->vhiwjk", row_sel, rpb, col_sel, precision=lax.Precision.HIGHEST)
    ok = (row_sel.sum(-1) > 0)[:, None, :, None, :, None] & col_ok[None, None, None, :, None, :]
    tab = jnp.where(ok, tab, NEG_BIG)
    return tab.reshape(3, rpb.shape[0], NA_GROUP_ROWS * GRID_W, NA_KEY_ROWS * GRID_W).astype(F32)


def _small_attn_kernel(q_ref, k_ref, v_ref, o_ref):
    q, k, v = q_ref[0], k_ref[0], v_ref[0]
    out = jnp.zeros(q.shape, F32)
    for mask in _head_masks(NA_WIDTH, NA_HEAD_DIM, NA_HEADS):
        s = _dot_t(jnp.where(mask, q, jnp.zeros_like(q)), k)
        p = jnp.exp(s - jnp.max(s, axis=-1, keepdims=True))
        o = _dot(p.astype(BF16), v)
        out = jnp.where(mask, o / jnp.sum(p, axis=-1, keepdims=True), out)
    o_ref[0] = out.astype(o_ref.dtype)


def _small_attn(q, k, v):
    B, N, W = q.shape
    spec = pl.BlockSpec((1, N, W), lambda b: (b, 0, 0))
    return pl.pallas_call(
        _small_attn_kernel, grid=(B,), in_specs=[spec, spec, spec], out_specs=spec,
        out_shape=jax.ShapeDtypeStruct((B, N, W), BF16),
        compiler_params=_cparams(("arbitrary",), 32),
        name="ctx_na_attention",
    )(q, k, v)


def _mix_out_kernel(x_ref, gt_ref, ya_ref, yb_ref, u_ref, up_ref, un_ref, wo_ref, wp_ref, ps_ref, o_ref,
                    *, per_seq, seq_len):
    tm = x_ref.shape[0]
    i = pl.program_id(0) % per_seq
    u = u_ref[...]
    prev = jnp.where(i == 0, 0.0, up_ref[...])
    nxt = jnp.where(i == per_seq - 1, 0.0, un_ref[...])
    ext = jnp.concatenate([prev, u, nxt], axis=0)
    n = tm + 2 * POOL_HALO
    sums = []
    p = ext + pltpu.roll(ext, 1, 0)
    sums.append(p)
    for step in (1, 2, 4):
        p = pltpu.roll(p, step, 0) + pltpu.roll(p, n - step, 0)
        sums.append(p)
    sums = [s[POOL_HALO:POOL_HALO + tm] for s in sums]
    lane = lax.broadcasted_iota(jnp.int32, (1, POOL_WIDTH), 1)
    grp = lane // POOL_GROUP_DIM
    win = jnp.where(grp == 0, sums[0], jnp.where(grp == 1, sums[1], jnp.where(grp == 2, sums[2], sums[3])))
    half = jnp.where(grp == 0, POOL_WINDOWS[0] // 2,
                     jnp.where(grp == 1, POOL_WINDOWS[1] // 2,
                               jnp.where(grp == 2, POOL_WINDOWS[2] // 2, POOL_WINDOWS[3] // 2)))
    t = i * tm + lax.broadcasted_iota(jnp.int32, (tm, 1), 0)
    cnt = jnp.minimum(t + half, seq_len) - jnp.maximum(t - half, 0)
    pooled = win / cnt.astype(F32) - u
    yc = _dot(pooled.astype(BF16), wp_ref[...]) * ps_ref[...]
    wa = wo_ref[0:MLA_WIDTH, :]
    wb = wo_ref[MLA_WIDTH:MLA_WIDTH + NA_WIDTH, :]
    wc = wo_ref[MLA_WIDTH + NA_WIDTH:, :]
    y = _dot(ya_ref[...], wa) + _dot(yb_ref[...], wb) + _dot(yc.astype(BF16), wc)
    o_ref[...] = x_ref[...] + gt_ref[0] * y


def _mix_out(x, gate, ya, yb, u, lw, seq_len):
    T, D = x.shape
    nb = gate.shape[0]
    tm = min(512, seq_len)
    per = (T // nb) // tm
    per_seq = seq_len // tm
    hb = tm // POOL_HALO
    last_hb = T // POOL_HALO - 1
    tok = lambda w: pl.BlockSpec((tm, w), lambda i: (i, 0))
    full = lambda a: pl.BlockSpec(a.shape, lambda i: (0,) * a.ndim)
    return pl.pallas_call(
        functools.partial(_mix_out_kernel, per_seq=per_seq, seq_len=seq_len),
        grid=(T // tm,),
        in_specs=[tok(D), pl.BlockSpec((1, 1, D), lambda i: (i // per, 0, 0)),
                  tok(MLA_WIDTH), tok(NA_WIDTH), tok(POOL_WIDTH),
                  pl.BlockSpec((POOL_HALO, POOL_WIDTH), lambda i: (jnp.maximum(i * hb - 1, 0), 0)),
                  pl.BlockSpec((POOL_HALO, POOL_WIDTH), lambda i: (jnp.minimum((i + 1) * hb, last_hb), 0)),
                  full(lw["w_out"]), full(lw["w_pool"]), full(lw["pool_scale"])],
        out_specs=tok(D),
        out_shape=jax.ShapeDtypeStruct((T, D), F32),
        compiler_params=_cparams(("arbitrary",), 48),
        name="mix_out",
    )(x, gate, ya, yb, u, u, u, lw["w_out"], lw["w_pool"], lw["pool_scale"])


def _rope_tables(n_tok):
    t = np.arange(n_tok)
    axis_dim = MLA_ROPE // 2
    nf = axis_dim // 2
    inv_freq = jnp.asarray(ROPE_BASE, F32) ** (-jnp.arange(0, axis_dim, 2, dtype=F32) / axis_dim)
    ar = jnp.asarray(t // GRID_W, F32)[:, None] * inv_freq
    ac = jnp.asarray(t % GRID_W, F32)[:, None] * inv_freq
    ones = jnp.ones((n_tok, MLA_NOPE), F32)
    zeros = jnp.zeros((n_tok, MLA_NOPE), F32)
    zf = jnp.zeros((n_tok, nf), F32)
    pad1 = jnp.ones((n_tok, MLA_PAD - MLA_QK), F32)
    pad0 = jnp.zeros((n_tok, MLA_PAD - MLA_QK), F32)
    cos = jnp.concatenate([ones, jnp.cos(ar), jnp.cos(ar), jnp.cos(ac), jnp.cos(ac), pad1], axis=1)
    sinp = jnp.concatenate([zeros, zf, jnp.sin(ar), zf, jnp.sin(ac), pad0], axis=1)
    sinm = jnp.concatenate([zeros, -jnp.sin(ar), zf, -jnp.sin(ac), zf, pad0], axis=1)
    return cos, sinp, sinm


def _identity_tables(n_tok):
    return (jnp.ones((n_tok, MLA_PAD), F32), jnp.zeros((n_tok, MLA_PAD), F32), jnp.zeros((n_tok, MLA_PAD), F32))


def _layer_weights(l, w_in, g_cq, w_q_up, g_ckv, w_kv_up, g_mla_q, g_mla_k, g_na_q, g_na_k,
                   w_pool, pool_scale, w_out):
    D = w_in.shape[1]
    wi = w_in[l]
    kr_cols = jnp.zeros((D, MLA_PAD), F32).at[:, MLA_NOPE:MLA_QK].set(wi[:, COL_KR:COL_KR + MLA_ROPE])
    w_in_ext = jnp.concatenate([wi[:, :COL_KR], kr_cols, wi[:, COL_KR + MLA_ROPE:]], axis=1).astype(BF16)
    wq = w_q_up[l].reshape(MLA_Q_RANK, MLA_HEADS, MLA_QK)
    wq = jnp.pad(wq, ((0, 0), (0, 0), (0, MLA_PAD - MLA_QK))).reshape(MLA_Q_RANK, MLA_HEADS * MLA_PAD)
    wkv = w_kv_up[l].reshape(MLA_KV_RANK, MLA_HEADS, MLA_NOPE + MLA_V)
    wk = jnp.pad(wkv[:, :, :MLA_NOPE], ((0, 0), (0, 0), (0, MLA_PAD - MLA_NOPE)))
    wv = wkv[:, :, MLA_NOPE:]
    w_kv = jnp.concatenate([wk.reshape(MLA_KV_RANK, -1), wv.reshape(MLA_KV_RANK, -1)], axis=1)
    pad_gain = lambda g: jnp.pad(g, (0, MLA_PAD - MLA_QK)).reshape(1, MLA_PAD)
    head_id = np.arange(NA_WIDTH) // NA_HEAD_DIM
    bd = jnp.asarray((head_id[:, None] == head_id[None, :]) / NA_HEAD_DIM, BF16)
    wp = jnp.zeros((POOL_WIDTH, POOL_WIDTH), F32)
    for g in range(len(POOL_WINDOWS)):
        sl = slice(g * POOL_GROUP_DIM, (g + 1) * POOL_GROUP_DIM)
        wp = wp.at[sl, sl].set(w_pool[l, g])
    return {
        "w_in": w_in_ext, "g_cq": g_cq[l].reshape(1, -1), "w_q": wq.astype(BF16),
        "g_ckv": g_ckv[l].reshape(1, -1), "w_kv": w_kv.astype(BF16),
        "g_q": pad_gain(g_mla_q[l]), "g_k": pad_gain(g_mla_k[l]),
        "bd": bd, "g_nq": jnp.tile(g_na_q[l], NA_HEADS).reshape(1, -1), "g_nk": jnp.tile(g_na_k[l], NA_HEADS).reshape(1, -1),
        "w_pool": wp.astype(BF16), "pool_scale": pool_scale[l].reshape(1, -1), "w_out": w_out[l].astype(BF16),
    }


def kernel(x, c, ctx, c_ctx, w_mod, b_mod, w_ffn_gate, w_ffn_up, w_ffn_down, w_in, g_cq, w_q_up, g_ckv, w_kv_up,
           g_mla_q, g_mla_k, g_na_q, g_na_k, na_rpb, w_pool, pool_scale, w_out):
    B, S, D = x.shape
    C = ctx.shape[1]
    L = w_mod.shape[0]
    assert S % (NA_GROUP_ROWS * GRID_W) == 0 and S // GRID_W >= NA_KEY_ROWS and B + 1 <= SUBLANES

    cc = jnp.zeros((SUBLANES, D), F32).at[:B].set(c).at[B].set(c_ctx)
    mods = _modulation(cc, w_mod, b_mod).reshape(L, SUBLANES, N_MOD, 1, D)
    rope = _rope_tables(S)
    no_rope = _identity_tables(C)
    wg, wu, wd = w_ffn_gate.astype(BF16), w_ffn_up.astype(BF16), w_ffn_down.astype(BF16)

    xs = x.reshape(B * S, D)
    xc = ctx.reshape(B * C, D)
    for l in range(L):
        last = l == L - 1
        mx = [mods[l, :B, j] for j in range(N_MOD)]
        mc = [mods[l, B:B + 1, j] for j in range(N_MOD)]
        lw = _layer_weights(l, w_in, g_cq, w_q_up, g_ckv, w_kv_up, g_mla_q, g_mla_k, g_na_q, g_na_k,
                            w_pool, pool_scale, w_out)
        xs = _ffn(xs, mx[0], mx[1], mx[2], wg[l, 0], wu[l, 0], wd[l, 0])
        xc = _ffn(xc, mc[0], mc[1], mc[2], wg[l, 0], wu[l, 0], wd[l, 0])

        qx, kx, vx, nqx, nkx, nvx, ux = _pre(xs, mx[3], mx[4], lw, rope, S)
        qc, kc, vc, nqc, nkc, nvc, uc = _pre(xc, mc[3], mc[4], lw, no_rope, C)
        per_b = lambda a, n: a.reshape(B, n, a.shape[-1])
        bound = _mla_score_bound(lw["g_q"], lw["g_k"])[0, 0]
        kv_c = (per_b(kc, C), per_b(vc, C))
        ya = _mla_attention(per_b(qx, S), [kv_c, (per_b(kx, S), per_b(vx, S))], bound)
        yb = _na(per_b(nqx, S), per_b(nkx, S), per_b(nvx, S), per_b(nkc, C), per_b(nvc, C),
                 _na_bias_tables(na_rpb[l]))
        xs = _mix_out(xs, mx[5], ya.reshape(B * S, -1), yb.reshape(B * S, -1), ux, lw, S)
        xs = _ffn(xs, mx[6], mx[7], mx[8], wg[l, 1], wu[l, 1], wd[l, 1])
        if not last:
            ya_c = _mla_attention(per_b(qc, C), [kv_c], bound)
            yb_c = _small_attn(per_b(nqc, C), per_b(nkc, C), per_b(nvc, C))
            xc = _mix_out(xc, mc[5], ya_c.reshape(B * C, -1), yb_c.reshape(B * C, -1), uc, lw, C)
            xc = _ffn(xc, mc[6], mc[7], mc[8], wg[l, 1], wu[l, 1], wd[l, 1])
    return xs.reshape(B, S, D)
```

```python
import functools
import math

import numpy as np
import jax
import jax.numpy as jnp
from jax import lax
from jax.experimental import pallas as pl
from jax.experimental.pallas import tpu as pltpu

F32 = jnp.float32
BF16 = jnp.bfloat16

LANES = 128
SUBLANES = 8

GRID_W = 64
MLA_HEADS = 8
MLA_NOPE = 64
MLA_ROPE = 32
MLA_QK = MLA_NOPE + MLA_ROPE
MLA_V = 64
MLA_Q_RANK = 768
MLA_KV_RANK = 256
MLA_WIDTH = MLA_HEADS * MLA_V
MLA_SCALE = 1.0 / math.sqrt(MLA_QK)
MLA_PAD = LANES
MLA_SCALE_LOG2 = MLA_SCALE * math.log2(math.e)
MLA_SHIFT_LANE = MLA_QK
MLA_FAST_BOUND = 60.0
MLA_KEY_TILE = 256
NA_HEADS = 4
NA_HEAD_DIM = 64
NA_WIDTH = NA_HEADS * NA_HEAD_DIM
NA_WIN_R = 8
NA_WIN_C = 16
NA_SCALE = 1.0 / math.sqrt(NA_HEAD_DIM)
NA_GROUP_ROWS = 4
NA_KEY_ROWS = NA_GROUP_ROWS + NA_WIN_R
POOL_WINDOWS = (2, 4, 8, 16)
POOL_GROUP_DIM = 64
POOL_WIDTH = len(POOL_WINDOWS) * POOL_GROUP_DIM
POOL_HALO = max(POOL_WINDOWS) // 2
FFN_CHUNK = 256
MIX_IN_ROWS = 1024
MIX_IN_SUB_ROWS = 256
N_MOD = 9
ROPE_BASE = 10000.0
EPS = 1e-6
NEG_BIG = -1e30

COL_CQ = 0
COL_CKV = COL_CQ + MLA_Q_RANK
COL_KR = COL_CKV + MLA_KV_RANK
COL_KRS = COL_KR + MLA_PAD
COL_NQ = COL_KRS + MLA_PAD
COL_NK = COL_NQ + NA_WIDTH
COL_NV = COL_NK + NA_WIDTH
COL_U = COL_NV + NA_WIDTH
IN_EXT = COL_U + POOL_WIDTH


def _dot(a, b):
    return jnp.dot(a, b, preferred_element_type=F32)


def _dot_t(a, b):
    return lax.dot_general(a, b, (((1,), (1,)), ((), ())), preferred_element_type=F32)


def _silu(x):
    return x * jax.nn.sigmoid(x)


def _cparams(sem, vmem_mb):
    return pltpu.CompilerParams(dimension_semantics=sem, vmem_limit_bytes=vmem_mb * 1024 * 1024)


def _mod_kernel(c_ref, w_ref, b_ref, o_ref):
    h = _silu(c_ref[...]).astype(BF16)
    o_ref[0] = _dot(h, w_ref[0].astype(BF16)) + b_ref[0]


def _modulation(cc, w_mod, b_mod):
    L, D, NM = w_mod.shape
    tn = 1024
    return pl.pallas_call(
        _mod_kernel,
        grid=(L, NM // tn),
        in_specs=[pl.BlockSpec((SUBLANES, D), lambda l, j: (0, 0)),
                  pl.BlockSpec((1, D, tn), lambda l, j: (l, 0, j)),
                  pl.BlockSpec((1, 1, tn), lambda l, j: (l, 0, j))],
        out_specs=pl.BlockSpec((1, SUBLANES, tn), lambda l, j: (l, 0, j)),
        out_shape=jax.ShapeDtypeStruct((L, SUBLANES, NM), F32),
        compiler_params=_cparams(("arbitrary", "arbitrary"), 32),
        name="modulation",
    )(cc, w_mod, b_mod.reshape(L, 1, NM))


def _modulated_norm(x, shift, scale):
    ms = jnp.mean(x * x, axis=-1, keepdims=True)
    return (x * lax.rsqrt(ms + EPS)) * (1.0 + scale) + shift


def _ffn_kernel(x_ref, sh_ref, sc_ref, gt_ref, wg_ref, wu_ref, wd_ref, o_ref):
    x = x_ref[...]
    h = _modulated_norm(x, sh_ref[0], sc_ref[0]).astype(BF16)
    acc = None
    for c in range(wg_ref.shape[1] // FFN_CHUNK):
        cols = slice(c * FFN_CHUNK, (c + 1) * FFN_CHUNK)
        g = _dot(h, wg_ref[:, cols])
        u = _dot(h, wu_ref[:, cols])
        y = _dot((_silu(g) * u).astype(BF16), wd_ref[cols, :])
        acc = y if acc is None else acc + y
    o_ref[...] = x + (0.5 * gt_ref[0]) * acc


def _resident(a):
    return pl.BlockSpec(a.shape, lambda i: (0,) * a.ndim, pipeline_mode=pl.Buffered(1))


def _ffn(x, shift, scale, gate, wg, wu, wd):
    T, D = x.shape
    nb = shift.shape[0]
    tm = min(512, T // nb)
    per = (T // nb) // tm
    row = lambda i: (i // per, 0, 0)
    return pl.pallas_call(
        _ffn_kernel,
        grid=(T // tm,),
        in_specs=[pl.BlockSpec((tm, D), lambda i: (i, 0)),
                  pl.BlockSpec((1, 1, D), row),
                  pl.BlockSpec((1, 1, D), row),
                  pl.BlockSpec((1, 1, D), row),
                  _resident(wg), _resident(wu), _resident(wd)],
        out_specs=pl.BlockSpec((tm, D), lambda i: (i, 0)),
        out_shape=jax.ShapeDtypeStruct((T, D), F32),
        compiler_params=_cparams(("arbitrary",), 48),
        name="ffn",
    )(x, shift, scale, gate, wg, wu, wd)


def _gain_norm(x, gain, width):
    ms = jnp.sum(x * x, axis=-1, keepdims=True) * (1.0 / width)
    return x * lax.rsqrt(ms + EPS) * gain


def _head_rsqrt(sumsq):
    return lax.rsqrt(sumsq * (1.0 / MLA_QK) + EPS)


def _mla_score_bound(gq, gk):
    amax = lambda g: jnp.max(jnp.abs(g), axis=-1, keepdims=True)
    return (MLA_SCALE_LOG2 * MLA_QK) * amax(gq) * amax(gk)


def _group_mean_square(x, bd):
    xx = x * x
    hi = xx.astype(BF16)
    lo = (xx - hi.astype(F32)).astype(BF16)
    return _dot(hi, bd) + _dot(lo, bd)


def _pre_kernel(x_ref, sh_ref, sc_ref, win_ref, gcq_ref, wq_ref, gckv_ref, wkv_ref, gq_ref, gqs_ref, gk_ref, gks_ref,
                cos_ref, sin_ref, bd_ref, gnq_ref, gnk_ref,
                q_ref, k_ref, v_ref, nq_ref, nk_ref, nv_ref, u_ref, *, sub_rows):
    lane = lax.broadcasted_iota(jnp.int32, (1, MLA_PAD), 1)
    real = lane < MLA_QK
    shift_lane = lane == MLA_SHIFT_LANE
    gq, gk = gq_ref[...], gk_ref[...]
    neg_bound = -_mla_score_bound(gq, gk)
    ones = jnp.ones((sub_rows, MLA_PAD), BF16)
    for s in range(x_ref.shape[0] // sub_rows):
        rows = slice(s * sub_rows, (s + 1) * sub_rows)
        h = _modulated_norm(x_ref[rows, :], sh_ref[0], sc_ref[0]).astype(BF16)
        z = _dot(h, win_ref[...])
        cos, sin = cos_ref[rows, :], sin_ref[rows, :]

        cq = _gain_norm(z[:, COL_CQ:COL_CKV], gcq_ref[...], MLA_Q_RANK).astype(BF16)
        qf = _dot(cq, wq_ref[...])
        q_cos = cos * (gq * MLA_SCALE_LOG2)
        q_sin = sin * (gqs_ref[...] * MLA_SCALE_LOG2)
        for hd in range(MLA_HEADS):
            sl = slice(hd * MLA_PAD, (hd + 1) * MLA_PAD)
            x = qf[:, sl]
            r = _head_rsqrt(jnp.sum(jnp.where(real, x * x, 0.0), axis=-1, keepdims=True))
            qh = (x * q_cos + pltpu.roll(x, MLA_PAD - MLA_ROPE, 1) * q_sin) * r
            q_ref[rows, sl] = jnp.where(shift_lane, neg_bound, qh).astype(BF16)

        ckv = _gain_norm(z[:, COL_CKV:COL_KR], gckv_ref[...], MLA_KV_RANK).astype(BF16)
        kvf = _dot(ckv, wkv_ref[...])
        kr = z[:, COL_KR:COL_KRS]
        k_rot = kr * (cos * gk) + z[:, COL_KRS:COL_NQ] * (sin * gks_ref[...])
        kr_sq = jnp.sum(kr * kr, axis=-1, keepdims=True)
        for hd in range(MLA_HEADS):
            sl = slice(hd * MLA_PAD, (hd + 1) * MLA_PAD)
            x = kvf[:, sl]
            r = _head_rsqrt(jnp.sum(x * x, axis=-1, keepdims=True) + kr_sq)
            k_ref[rows, sl] = jnp.where(shift_lane, 1.0, (x * gk + k_rot) * r).astype(BF16)
        for pr in range(MLA_HEADS // 2):
            vcols = slice(MLA_HEADS * MLA_PAD + pr * MLA_PAD, MLA_HEADS * MLA_PAD + (pr + 1) * MLA_PAD)
            v_ref[rows, 2 * pr * MLA_PAD:(2 * pr + 1) * MLA_PAD] = kvf[:, vcols].astype(BF16)
            v_ref[rows, (2 * pr + 1) * MLA_PAD:(2 * pr + 2) * MLA_PAD] = ones

        bd = bd_ref[...]
        nq = z[:, COL_NQ:COL_NK]
        nq_ref[rows, :] = (nq * lax.rsqrt(_group_mean_square(nq, bd) + EPS) * (gnq_ref[...] * NA_SCALE)).astype(BF16)
        nk = z[:, COL_NK:COL_NV]
        nk_ref[rows, :] = (nk * lax.rsqrt(_group_mean_square(nk, bd) + EPS) * gnk_ref[...]).astype(BF16)
        nv_ref[rows, :] = z[:, COL_NV:COL_U].astype(BF16)
        u_ref[rows, :] = z[:, COL_U:IN_EXT]


def _pre(x, shift, scale, lw, tables, seq_len):
    T, D = x.shape
    nb = shift.shape[0]
    tm = min(MIX_IN_ROWS, seq_len)
    per = (T // nb) // tm
    per_seq = seq_len // tm
    row = lambda i: (i // per, 0, 0)
    tok = lambda w: pl.BlockSpec((tm, w), lambda i: (i, 0))
    full = _resident
    pos =pl.BlockSpec((tm, LANES), lambda i: (i % per_seq, 0))
    weights = (lw["w_in"], lw["g_cq"], lw["w_q"], lw["g_ckv"], lw["w_kv"],
               lw["g_q"], lw["g_q_partner"], lw["g_k"], lw["g_k_partner"])
    tail = (lw["bd"], lw["g_nq"], lw["g_nk"])
    hq = MLA_HEADS * MLA_PAD
    return pl.pallas_call(
        functools.partial(_pre_kernel, sub_rows=min(MIX_IN_SUB_ROWS, tm)),
        grid=(T // tm,),
        in_specs=[tok(D), pl.BlockSpec((1, 1, D), row), pl.BlockSpec((1, 1, D), row)]
                 + [full(a) for a in weights] + [pos, pos] + [full(a) for a in tail],
        out_specs=[tok(hq), tok(hq), tok(hq), tok(NA_WIDTH), tok(NA_WIDTH), tok(NA_WIDTH), tok(POOL_WIDTH)],
        out_shape=[jax.ShapeDtypeStruct((T, hq), BF16), jax.ShapeDtypeStruct((T, hq), BF16),
                   jax.ShapeDtypeStruct((T, hq), BF16), jax.ShapeDtypeStruct((T, NA_WIDTH), BF16),
                   jax.ShapeDtypeStruct((T, NA_WIDTH), BF16), jax.ShapeDtypeStruct((T, NA_WIDTH), BF16),
                   jax.ShapeDtypeStruct((T, POOL_WIDTH), F32)],
        compiler_params=_cparams(("arbitrary",), 56),
        name="mix_in",
    )(x, shift, scale, *weights, *tables, *tail)


def _flash_fast_kernel(q_ref, *refs):
    o_ref = refs[-1]
    tq = q_ref.shape[1]
    for hd in range(2):
        lanes = slice(hd * MLA_PAD, (hd + 1) * MLA_PAD)
        q = q_ref[0, :, lanes]
        acc = jnp.zeros((tq, 2 * MLA_PAD), F32)
        for k_ref, v_ref in zip(refs[0:-1:2], refs[1:-1:2]):
            for t in range(k_ref.shape[1] // MLA_KEY_TILE):
                keys = slice(t * MLA_KEY_TILE, (t + 1) * MLA_KEY_TILE)
                p = jnp.exp2(_dot_t(q, k_ref[0, keys, lanes])).astype(BF16)
                acc = acc + _dot(p, v_ref[0, keys, :])
        own = slice(hd * MLA_V, (hd + 1) * MLA_V)
        o_ref[0, :, own] = (acc[:, own] / acc[:, MLA_PAD:MLA_PAD + 1]).astype(o_ref.dtype)


def _flash_fast(q, kv):
    B, Nq, HQ = q.shape
    pairs = HQ // (2 * MLA_PAD)
    tq = min(1024, Nq)
    kv_specs = [pl.BlockSpec((1, a.shape[1], 2 * MLA_PAD), lambda b, h, i: (b, 0, h)) for pair in kv for a in pair]
    return pl.pallas_call(
        _flash_fast_kernel,
        grid=(B, pairs, Nq // tq),
        in_specs=[pl.BlockSpec((1, tq, 2 * MLA_PAD), lambda b, h, i: (b, i, h))] + kv_specs,
        out_specs=pl.BlockSpec((1, tq, 2 * MLA_V), lambda b, h, i: (b, i, h)),
        out_shape=jax.ShapeDtypeStruct((B, Nq, pairs * 2 * MLA_V), BF16),
        compiler_params=_cparams(("arbitrary", "arbitrary", "arbitrary"), 56),
        name="mla_attention",
    )(q, *[a for pair in kv for a in pair])


def _flash_safe_kernel(q_ref, k_ref, v_ref, o_ref, *, tk):
    tq = q_ref.shape[1]
    nk = k_ref.shape[1] // tk
    for hd in range(2):
        lanes = slice(hd * MLA_PAD, (hd + 1) * MLA_PAD)
        q = q_ref[0, :, lanes]

        def body(j, carry):
            m, acc = carry
            start = pl.multiple_of(j * tk, tk)
            s = _dot_t(q, k_ref[0, pl.ds(start, tk), lanes])
            m_new = jnp.maximum(m, jnp.max(s, axis=-1, keepdims=True))
            p = jnp.exp2(s - m_new)
            acc = jnp.exp2(m - m_new) * acc + _dot(p.astype(BF16), v_ref[0, pl.ds(start, tk), :])
            return m_new, acc

        init = (jnp.full((tq, 1), NEG_BIG, F32), jnp.zeros((tq, 2 * MLA_PAD), F32))
        _, acc = lax.fori_loop(0, nk, body, init)
        own = slice(hd * MLA_V, (hd + 1) * MLA_V)
        o_ref[0, :, own] = (acc[:, own] / acc[:, MLA_PAD:MLA_PAD + 1]).astype(o_ref.dtype)


def _flash_safe(q, kv):
    k = jnp.concatenate([pair[0] for pair in kv], axis=1)
    v = jnp.concatenate([pair[1] for pair in kv], axis=1)
    B, Nq, HQ = q.shape
    Nk = k.shape[1]
    pairs = HQ // (2 * MLA_PAD)
    tq = min(512, Nq)
    tk = next(t for t in (768, 512, 256) if Nk % t == 0)
    kv_spec = pl.BlockSpec((1, Nk, 2 * MLA_PAD), lambda b, h, i: (b, 0, h))
    return pl.pallas_call(
        functools.partial(_flash_safe_kernel, tk=tk),
        grid=(B, pairs, Nq // tq),
        in_specs=[pl.BlockSpec((1, tq, 2 * MLA_PAD), lambda b, h, i: (b, i, h)), kv_spec, kv_spec],
        out_specs=pl.BlockSpec((1, tq, 2 * MLA_V), lambda b, h, i: (b, i, h)),
        out_shape=jax.ShapeDtypeStruct((B, Nq, pairs * 2 * MLA_V), BF16),
        compiler_params=_cparams(("arbitrary", "arbitrary", "arbitrary"), 56),
        name="mla_attention_running_max",
    )(q, k, v)


def _mla_attention(q, kv, bound):
    return lax.cond(bound <= MLA_FAST_BOUND, lambda: _flash_fast(q, kv), lambda: _flash_safe(q, kv))


def _head_masks(width, head_dim, heads):
    lane = lax.broadcasted_iota(jnp.int32, (1, width), 1)
    return [(lane >= h * head_dim) & (lane < (h + 1) * head_dim) for h in range(heads)]


def _na_kernel(q_ref, k_ref, v_ref, kc_ref, vc_ref, bias_ref, o_ref):
    g = pl.program_id(1)
    rows = k_ref.shape[1] // GRID_W
    base = jnp.clip(g * NA_GROUP_ROWS - NA_WIN_R // 2, 0, rows - NA_KEY_ROWS)
    start = pl.multiple_of(base * GRID_W, GRID_W)
    kw = k_ref[0, pl.ds(start, NA_KEY_ROWS * GRID_W), :]
    vw = v_ref[0, pl.ds(start, NA_KEY_ROWS * GRID_W), :]
    kc, vc = kc_ref[0], vc_ref[0]
    q = q_ref[0]
    out = jnp.zeros(q.shape, F32)
    for hd, mask in enumerate(_head_masks(NA_WIDTH, NA_HEAD_DIM, NA_HEADS)):
        qh = jnp.where(mask, q, jnp.zeros_like(q))
        s_loc = _dot_t(qh, kw) + bias_ref[0, hd]
        s_ctx = _dot_t(qh, kc)
        m = jnp.maximum(jnp.max(s_loc, axis=-1, keepdims=True), jnp.max(s_ctx, axis=-1, keepdims=True))
        p_loc = jnp.exp(s_loc - m)
        p_ctx = jnp.exp(s_ctx - m)
        l = jnp.sum(p_loc, axis=-1, keepdims=True) + jnp.sum(p_ctx, axis=-1, keepdims=True)
        o = _dot(p_loc.astype(BF16), vw) + _dot(p_ctx.astype(BF16), vc)
        out = jnp.where(mask, o / l, out)
    o_ref[0] = out.astype(o_ref.dtype)


def _na(q, k, v, kc, vc, bias):
    B, S, W = q.shape
    C = kc.shape[1]
    tq = NA_GROUP_ROWS * GRID_W
    groups = S // tq

    def variant(b, g):
        return (jnp.where(g == 0, 0, jnp.where(g == groups - 1, 2, 1)), 0, 0, 0)

    return pl.pallas_call(
        _na_kernel,
        grid=(B, groups),
        in_specs=[pl.BlockSpec((1, tq, W), lambda b, g: (b, g, 0)),
                  pl.BlockSpec((1, S, W), lambda b, g: (b, 0, 0)),
                  pl.BlockSpec((1, S, W), lambda b, g: (b, 0, 0)),
                  pl.BlockSpec((1, C, W), lambda b, g: (b, 0, 0)),
                  pl.BlockSpec((1, C, W), lambda b, g: (b, 0, 0)),
                  pl.BlockSpec((1,) + bias.shape[1:], variant)],
        out_specs=pl.BlockSpec((1, tq, W), lambda b, g: (b, g, 0)),
        out_shape=jax.ShapeDtypeStruct((B, S, W), BF16),
        compiler_params=_cparams(("arbitrary", "arbitrary"), 56),
        name="na_attention",
    )(q, k, v, kc, vc, bias)


def _na_bias_tables(rpb):
    i = np.arange(NA_GROUP_ROWS)[:, None]
    jj = np.arange(NA_KEY_ROWS)[None, :]
    half = NA_WIN_R // 2
    q_rel = (i, half + i, NA_WIN_R + i)
    r0_rel = (0 * i, i, half + 0 * i)
    n_r, n_c = 2 * NA_WIN_R - 1, 2 * NA_WIN_C - 1
    row_sel = np.zeros((3, NA_GROUP_ROWS, NA_KEY_ROWS, n_r), np.float32)
    for v, (qr, r0) in enumerate(zip(q_rel, r0_rel)):
        row_ok = (jj >= r0) & (jj < r0 + NA_WIN_R)
        row_off = jj - qr + NA_WIN_R - 1
        row_sel[v] = row_ok[..., None] & (row_off[..., None] == np.arange(n_r))
    w = np.arange(GRID_W)[:, None]
    kcol = np.arange(GRID_W)[None, :]
    cs = np.clip(w - NA_WIN_C // 2, 0, GRID_W - NA_WIN_C)
    col_ok = (kcol >= cs) & (kcol < cs + NA_WIN_C)
    col_sel = (col_ok[..., None] & ((kcol - w + NA_WIN_C - 1)[..., None] == np.arange(n_c))).astype(np.float32)
    tab = jnp.einsum("vijr,hrc,wkc->vhiwjk", row_sel, rpb, col_sel, precision=lax.Precision.HIGHEST)
    ok = (row_sel.sum(-1) > 0)[:, None, :, None, :, None] & col_ok[None, None, None, :, None, :]
    tab = jnp.where(ok, tab, NEG_BIG)
    return tab.reshape(3, rpb.shape[0], NA_GROUP_ROWS * GRID_W, NA_KEY_ROWS * GRID_W).astype(F32)


def _small_attn_kernel(q_ref, k_ref, v_ref, o_ref):
    q, k, v = q_ref[0], k_ref[0], v_ref[0]
    out = jnp.zeros(q.shape, F32)
    for mask in _head_masks(NA_WIDTH, NA_HEAD_DIM, NA_HEADS):
        s = _dot_t(jnp.where(mask, q, jnp.zeros_like(q)), k)
        p = jnp.exp(s - jnp.max(s, axis=-1, keepdims=True))
        o = _dot(p.astype(BF16), v)
        out = jnp.where(mask, o / jnp.sum(p, axis=-1, keepdims=True), out)
    o_ref[0] = out.astype(o_ref.dtype)


def _small_attn(q, k, v):
    B, N, W = q.shape
    spec = pl.BlockSpec((1, N, W), lambda b: (b, 0, 0))
    return pl.pallas_call(
        _small_attn_kernel, grid=(B,), in_specs=[spec, spec, spec], out_specs=spec,
        out_shape=jax.ShapeDtypeStruct((B, N, W), BF16),
        compiler_params=_cparams(("arbitrary",), 32),
        name="ctx_na_attention",
    )(q, k, v)


def _mix_out_kernel(x_ref, gt_ref, ya_ref, yb_ref, u_ref, up_ref, un_ref, wo_ref, wp_ref, ps_ref, o_ref,
                    *, per_seq, seq_len):
    tm = x_ref.shape[0]
    i = pl.program_id(0) % per_seq
    u = u_ref[...]
    prev = jnp.where(i == 0, 0.0, up_ref[...])
    nxt = jnp.where(i == per_seq - 1, 0.0, un_ref[...])
    ext = jnp.concatenate([prev, u, nxt], axis=0)
    n = tm + 2 * POOL_HALO
    sums = []
    p = ext + pltpu.roll(ext, 1, 0)
    sums.append(p)
    for step in (1, 2, 4):
        p = pltpu.roll(p, step, 0) + pltpu.roll(p, n - step, 0)
        sums.append(p)
    sums = [s[POOL_HALO:POOL_HALO + tm] for s in sums]
    lane = lax.broadcasted_iota(jnp.int32, (1, POOL_WIDTH), 1)
    grp = lane // POOL_GROUP_DIM
    win = jnp.where(grp == 0, sums[0], jnp.where(grp == 1, sums[1], jnp.where(grp == 2, sums[2], sums[3])))
    half = jnp.where(grp == 0, POOL_WINDOWS[0] // 2,
                     jnp.where(grp == 1, POOL_WINDOWS[1] // 2,
                               jnp.where(grp == 2, POOL_WINDOWS[2] // 2, POOL_WINDOWS[3] // 2)))
    t = i * tm + lax.broadcasted_iota(jnp.int32, (tm, 1), 0)
    cnt = jnp.minimum(t + half, seq_len) - jnp.maximum(t - half, 0)
    pooled = win / cnt.astype(F32) - u
    yc = _dot(pooled.astype(BF16), wp_ref[...]) * ps_ref[...]
    wa = wo_ref[0:MLA_WIDTH, :]
    wb = wo_ref[MLA_WIDTH:MLA_WIDTH + NA_WIDTH, :]
    wc = wo_ref[MLA_WIDTH + NA_WIDTH:, :]
    y = _dot(ya_ref[...], wa) + _dot(yb_ref[...], wb) + _dot(yc.astype(BF16), wc)
    o_ref[...] = x_ref[...] + gt_ref[0] * y


def _mix_out(x, gate, ya, yb, u, lw, seq_len):
    T, D = x.shape
    nb = gate.shape[0]
    tm = min(512, seq_len)
    per = (T // nb) // tm
    per_seq = seq_len // tm
    hb = tm // POOL_HALO
    last_hb = T // POOL_HALO - 1
    tok = lambda w: pl.BlockSpec((tm, w), lambda i: (i, 0))
    full = _resident
    return pl.pallas_call(
        functools.partial(_mix_out_kernel, per_seq=per_seq, seq_len=seq_len),
        grid=(T // tm,),
        in_specs=[tok(D), pl.BlockSpec((1, 1, D), lambda i: (i // per, 0, 0)),
                  tok(MLA_WIDTH), tok(NA_WIDTH), tok(POOL_WIDTH),
                  pl.BlockSpec((POOL_HALO, POOL_WIDTH), lambda i: (jnp.maximum(i * hb - 1, 0), 0)),
                  pl.BlockSpec((POOL_HALO, POOL_WIDTH), lambda i: (jnp.minimum((i + 1) * hb, last_hb), 0)),
                  full(lw["w_out"]), full(lw["w_pool"]), full(lw["pool_scale"])],
        out_specs=tok(D),
        out_shape=jax.ShapeDtypeStruct((T, D), F32),
        compiler_params=_cparams(("arbitrary",), 48),
        name="mix_out",
    )(x, gate, ya, yb, u, u, u, lw["w_out"], lw["w_pool"], lw["pool_scale"])


def _rope_partner(a):
    lead = a.shape[:-1]
    return a.reshape(lead + (2, 2, MLA_ROPE // 4))[..., ::-1, :].reshape(lead + (MLA_ROPE,))


def _rope_tables(n_tok):
    t = np.arange(n_tok)
    axis_dim = MLA_ROPE // 2
    inv_freq = jnp.asarray(ROPE_BASE, F32) ** (-jnp.arange(0, axis_dim, 2, dtype=F32) / axis_dim)
    ar = jnp.asarray(t // GRID_W, F32)[:, None] * inv_freq
    ac = jnp.asarray(t % GRID_W, F32)[:, None] * inv_freq
    ones = jnp.ones((n_tok, MLA_NOPE), F32)
    zeros = jnp.zeros((n_tok, MLA_NOPE), F32)
    pad = jnp.zeros((n_tok, MLA_PAD - MLA_QK), F32)
    cos = jnp.concatenate([ones, jnp.cos(ar), jnp.cos(ar), jnp.cos(ac), jnp.cos(ac), pad + 1.0], axis=1)
    sin = jnp.concatenate([zeros, -jnp.sin(ar), jnp.sin(ar), -jnp.sin(ac), jnp.sin(ac), pad], axis=1)
    return cos, sin


def _identity_tables(n_tok):
    return jnp.ones((n_tok, MLA_PAD), F32), jnp.zeros((n_tok, MLA_PAD), F32)


def _layer_weights(l, w_in, g_cq, w_q_up, g_ckv, w_kv_up, g_mla_q, g_mla_k, g_na_q, g_na_k,
                   w_pool, pool_scale, w_out):
    D = w_in.shape[1]
    wi = w_in[l]
    w_kr = wi[:, COL_KR:COL_KR + MLA_ROPE]
    in_rope_lanes = lambda w: jnp.pad(w, ((0, 0), (MLA_NOPE, MLA_PAD - MLA_QK)))
    w_in_ext = jnp.concatenate([wi[:, :COL_KR], in_rope_lanes(w_kr), in_rope_lanes(_rope_partner(w_kr)),
                                wi[:, COL_KR + MLA_ROPE:]], axis=1).astype(BF16)
    wq = w_q_up[l].reshape(MLA_Q_RANK, MLA_HEADS, MLA_QK)
    wq = jnp.concatenate([wq, _rope_partner(wq[:, :, MLA_NOPE:])], axis=2).reshape(MLA_Q_RANK, MLA_HEADS * MLA_PAD)
    wkv = w_kv_up[l].reshape(MLA_KV_RANK, MLA_HEADS, MLA_NOPE + MLA_V)
    wk = jnp.pad(wkv[:, :, :MLA_NOPE], ((0, 0), (0, 0), (0, MLA_PAD - MLA_NOPE)))
    wv = wkv[:, :, MLA_NOPE:]
    w_kv = jnp.concatenate([wk.reshape(MLA_KV_RANK, -1), wv.reshape(MLA_KV_RANK, -1)], axis=1)
    pad_gain = lambda g: jnp.pad(g, (0, MLA_PAD - MLA_QK)).reshape(1, MLA_PAD)
    partner_gain = lambda g: jnp.pad(_rope_partner(g[MLA_NOPE:]), (MLA_NOPE, MLA_PAD - MLA_QK)).reshape(1, MLA_PAD)
    head_id = np.arange(NA_WIDTH) // NA_HEAD_DIM
    bd = jnp.asarray((head_id[:, None] == head_id[None, :]) / NA_HEAD_DIM, BF16)
    wp = jnp.zeros((POOL_WIDTH, POOL_WIDTH), F32)
    for g in range(len(POOL_WINDOWS)):
        sl = slice(g * POOL_GROUP_DIM, (g + 1) * POOL_GROUP_DIM)
        wp = wp.at[sl, sl].set(w_pool[l, g])
    return {
        "w_in": w_in_ext, "g_cq": g_cq[l].reshape(1, -1), "w_q": wq.astype(BF16),
        "g_ckv": g_ckv[l].reshape(1, -1), "w_kv": w_kv.astype(BF16),
        "g_q": pad_gain(g_mla_q[l]), "g_k": pad_gain(g_mla_k[l]),
        "g_q_partner": partner_gain(g_mla_q[l]), "g_k_partner": partner_gain(g_mla_k[l]),
        "bd": bd, "g_nq": jnp.tile(g_na_q[l], NA_HEADS).reshape(1, -1), "g_nk": jnp.tile(g_na_k[l], NA_HEADS).reshape(1, -1),
        "w_pool": wp.astype(BF16), "pool_scale": pool_scale[l].reshape(1, -1), "w_out": w_out[l].astype(BF16),
    }


def kernel(x, c, ctx, c_ctx, w_mod, b_mod, w_ffn_gate, w_ffn_up, w_ffn_down, w_in, g_cq, w_q_up, g_ckv, w_kv_up,
           g_mla_q, g_mla_k, g_na_q, g_na_k, na_rpb, w_pool, pool_scale, w_out):
    B, S, D = x.shape
    C = ctx.shape[1]
    L = w_mod.shape[0]
    assert S % (NA_GROUP_ROWS * GRID_W) == 0 and S // GRID_W >= NA_KEY_ROWS and B + 1 <= SUBLANES

    cc = jnp.zeros((SUBLANES, D), F32).at[:B].set(c).at[B].set(c_ctx)
    mods = _modulation(cc, w_mod, b_mod).reshape(L, SUBLANES, N_MOD, 1, D)
    rope = _rope_tables(S)
    no_rope = _identity_tables(C)
    wg, wu, wd = w_ffn_gate.astype(BF16), w_ffn_up.astype(BF16), w_ffn_down.astype(BF16)

    xs = x.reshape(B * S, D)
    xc = ctx.reshape(B * C, D)
    for l in range(L):
        last = l == L - 1
        mx = [mods[l, :B, j] for j in range(N_MOD)]
        mc = [mods[l, B:B + 1, j] for j in range(N_MOD)]
        lw = _layer_weights(l, w_in, g_cq, w_q_up, g_ckv, w_kv_up, g_mla_q, g_mla_k, g_na_q, g_na_k,
                            w_pool, pool_scale, w_out)
        xs = _ffn(xs, mx[0], mx[1], mx[2], wg[l, 0], wu[l, 0], wd[l, 0])
        xc = _ffn(xc, mc[0], mc[1], mc[2], wg[l, 0], wu[l, 0], wd[l, 0])

        qx, kx, vx, nqx, nkx, nvx, ux = _pre(xs, mx[3], mx[4], lw, rope, S)
        qc, kc, vc, nqc, nkc, nvc, uc = _pre(xc, mc[3], mc[4], lw, no_rope, C)
        per_b = lambda a, n: a.reshape(B, n, a.shape[-1])
        bound = _mla_score_bound(lw["g_q"], lw["g_k"])[0, 0]
        kv_c = (per_b(kc, C), per_b(vc, C))
        ya = _mla_attention(per_b(qx, S), [kv_c, (per_b(kx, S), per_b(vx, S))], bound)
        yb = _na(per_b(nqx, S), per_b(nkx, S), per_b(nvx, S), per_b(nkc, C), per_b(nvc, C),
                 _na_bias_tables(na_rpb[l]))
        xs = _mix_out(xs, mx[5], ya.reshape(B * S, -1), yb.reshape(B * S, -1), ux, lw, S)
        xs = _ffn(xs, mx[6], mx[7], mx[8], wg[l, 1], wu[l, 1], wd[l, 1])
        if not last:
            ya_c = _mla_attention(per_b(qc, C), [kv_c], bound)
            yb_c = _small_attn(per_b(nqc, C), per_b(nkc, C), per_b(nvc, C))
            xc = _mix_out(xc, mc[5], ya_c.reshape(B * C, -1), yb_c.reshape(B * C, -1), uc, lw, C)
            xc = _ffn(xc, mc[6], mc[7], mc[8], wg[l, 1], wu[l, 1], wd[l, 1])
    return xs.reshape(B, S, D)
```

```python
import functools
import math

import numpy as np
import jax
import jax.numpy as jnp
from jax import lax
from jax.experimental import pallas as pl
from jax.experimental.pallas import tpu as pltpu

F32 = jnp.float32
BF16 = jnp.bfloat16

LANES = 128
SUBLANES = 8

GRID_W = 64
MLA_HEADS = 8
MLA_NOPE = 64
MLA_ROPE = 32
MLA_QK = MLA_NOPE + MLA_ROPE
MLA_V = 64
MLA_Q_RANK = 768
MLA_KV_RANK = 256
MLA_WIDTH = MLA_HEADS * MLA_V
MLA_SCALE = 1.0 / math.sqrt(MLA_QK)
MLA_PAD = LANES
MLA_SCALE_LOG2 = MLA_SCALE * math.log2(math.e)
MLA_SHIFT_LANE = MLA_QK
MLA_FAST_BOUND = 60.0
MLA_KEY_TILE = 256
NA_HEADS = 4
NA_HEAD_DIM = 64
NA_WIDTH = NA_HEADS * NA_HEAD_DIM
NA_WIN_R = 8
NA_WIN_C = 16
NA_SCALE = 1.0 / math.sqrt(NA_HEAD_DIM)
NA_GROUP_ROWS = 4
NA_KEY_ROWS = NA_GROUP_ROWS + NA_WIN_R
POOL_WINDOWS = (2, 4, 8, 16)
POOL_GROUP_DIM = 64
POOL_WIDTH = len(POOL_WINDOWS) * POOL_GROUP_DIM
POOL_HALO = max(POOL_WINDOWS) // 2
FFN_CHUNK = 256
MIX_IN_ROWS = 1024
MIX_IN_SUB_ROWS = 256
N_MOD = 9
ROPE_BASE = 10000.0
EPS = 1e-6
NEG_BIG = -1e30

COL_CQ = 0
COL_CKV = COL_CQ + MLA_Q_RANK
COL_KR = COL_CKV + MLA_KV_RANK
COL_KRS = COL_KR + MLA_PAD
COL_NQ = COL_KRS + MLA_PAD
COL_NK = COL_NQ + NA_WIDTH
COL_NV = COL_NK + NA_WIDTH
COL_U = COL_NV + NA_WIDTH
IN_EXT = COL_U + POOL_WIDTH


def _dot(a, b):
    return jnp.dot(a, b, preferred_element_type=F32)


def _dot_t(a, b):
    return lax.dot_general(a, b, (((1,), (1,)), ((), ())), preferred_element_type=F32)


def _silu(x):
    return x * jax.nn.sigmoid(x)


def _cparams(sem, vmem_mb):
    return pltpu.CompilerParams(dimension_semantics=sem, vmem_limit_bytes=vmem_mb * 1024 * 1024)


def _mod_kernel(c_ref, w_ref, b_ref, o_ref):
    h = _silu(c_ref[...]).astype(BF16)
    o_ref[0] = _dot(h, w_ref[0].astype(BF16)) + b_ref[0]


def _modulation(cc, w_mod, b_mod):
    L, D, NM = w_mod.shape
    tn = 1024
    return pl.pallas_call(
        _mod_kernel,
        grid=(L, NM // tn),
        in_specs=[pl.BlockSpec((SUBLANES, D), lambda l, j: (0, 0)),
                  pl.BlockSpec((1, D, tn), lambda l, j: (l, 0, j)),
                  pl.BlockSpec((1, 1, tn), lambda l, j: (l, 0, j))],
        out_specs=pl.BlockSpec((1, SUBLANES, tn), lambda l, j: (l, 0, j)),
        out_shape=jax.ShapeDtypeStruct((L, SUBLANES, NM), F32),
        compiler_params=_cparams(("arbitrary", "arbitrary"), 32),
        name="modulation",
    )(cc, w_mod, b_mod.reshape(L, 1, NM))


def _modulated_norm(x, shift, scale):
    ms = jnp.mean(x * x, axis=-1, keepdims=True)
    return (x * lax.rsqrt(ms + EPS)) * (1.0 + scale) + shift


def _ffn_kernel(x_ref, sh_ref, sc_ref, gt_ref, wg_ref, wu_ref, wd_ref, o_ref):
    x = x_ref[...]
    h = _modulated_norm(x, sh_ref[0], sc_ref[0]).astype(BF16)
    acc = None
    for c in range(wg_ref.shape[1] // FFN_CHUNK):
        cols = slice(c * FFN_CHUNK, (c + 1) * FFN_CHUNK)
        g = _dot(h, wg_ref[:, cols])
        u = _dot(h, wu_ref[:, cols])
        y = _dot((_silu(g) * u).astype(BF16), wd_ref[cols, :])
        acc = y if acc is None else acc + y
    o_ref[...] = x + (0.5 * gt_ref[0]) * acc


def _resident(a):
    return pl.BlockSpec(a.shape, lambda i: (0,) * a.ndim, pipeline_mode=pl.Buffered(1))


def _mod_spec(mod, j, per):
    table, first, _ = mod
    return pl.BlockSpec((1, 1, table.shape[-1]), lambda i: (first + (i // per) * N_MOD + j, 0, 0))


def _ffn(x, mod, js, weights, layer, half):
    T, D = x.shape
    nb = mod[2]
    tm = min(512, T // nb)
    per = (T // nb) // tm
    whole = lambda a: pl.BlockSpec((None, None) + a.shape[2:], lambda i: (layer, half, 0, 0),
                                   pipeline_mode=pl.Buffered(1))
    return pl.pallas_call(
        _ffn_kernel,
        grid=(T // tm,),
        in_specs=[pl.BlockSpec((tm, D), lambda i: (i, 0))] + [_mod_spec(mod, j, per) for j in js]
                 + [whole(a) for a in weights],
        out_specs=pl.BlockSpec((tm, D), lambda i: (i, 0)),
        out_shape=jax.ShapeDtypeStruct((T, D), F32),
        compiler_params=_cparams(("arbitrary",), 48),
        name="ffn",
    )(x, mod[0], mod[0], mod[0], *weights)


def _gain_norm(x, gain, width):
    ms = jnp.sum(x * x, axis=-1, keepdims=True) * (1.0 / width)
    return x * lax.rsqrt(ms + EPS) * gain


def _head_rsqrt(sumsq):
    return lax.rsqrt(sumsq * (1.0 / MLA_QK) + EPS)


def _mla_score_bound(gq, gk):
    amax = lambda g: jnp.max(jnp.abs(g), axis=-1, keepdims=True)
    return (MLA_SCALE_LOG2 * MLA_QK) * amax(gq) * amax(gk)


def _group_mean_square(x, bd):
    xx = x * x
    hi = xx.astype(BF16)
    lo = (xx - hi.astype(F32)).astype(BF16)
    return _dot(hi, bd) + _dot(lo, bd)


def _pre_kernel(x_ref, sh_ref, sc_ref, win_ref, gcq_ref, wq_ref, gckv_ref, wkv_ref, gq_ref, gqs_ref, gk_ref, gks_ref,
                cos_ref, sin_ref, bd_ref, gnq_ref, gnk_ref,
                q_ref, k_ref, v_ref, nq_ref, nk_ref, nv_ref, u_ref, *, sub_rows):
    lane = lax.broadcasted_iota(jnp.int32, (1, MLA_PAD), 1)
    real = lane < MLA_QK
    shift_lane = lane == MLA_SHIFT_LANE
    gq, gk = gq_ref[...], gk_ref[...]
    neg_bound = -_mla_score_bound(gq, gk)
    ones = jnp.ones((sub_rows, MLA_PAD), BF16)
    for s in range(x_ref.shape[0] // sub_rows):
        rows = slice(s * sub_rows, (s + 1) * sub_rows)
        h = _modulated_norm(x_ref[rows, :], sh_ref[0], sc_ref[0]).astype(BF16)
        z = _dot(h, win_ref[...])
        cos, sin = cos_ref[rows, :], sin_ref[rows, :]

        cq = _gain_norm(z[:, COL_CQ:COL_CKV], gcq_ref[...], MLA_Q_RANK).astype(BF16)
        qf = _dot(cq, wq_ref[...])
        q_cos = cos * (gq * MLA_SCALE_LOG2)
        q_sin = sin * (gqs_ref[...] * MLA_SCALE_LOG2)
        for hd in range(MLA_HEADS):
            sl = slice(hd * MLA_PAD, (hd + 1) * MLA_PAD)
            x = qf[:, sl]
            r = _head_rsqrt(jnp.sum(jnp.where(real, x * x, 0.0), axis=-1, keepdims=True))
            qh = (x * q_cos + pltpu.roll(x, MLA_PAD - MLA_ROPE, 1) * q_sin) * r
            q_ref[rows, sl] = jnp.where(shift_lane, neg_bound, qh).astype(BF16)

        ckv = _gain_norm(z[:, COL_CKV:COL_KR], gckv_ref[...], MLA_KV_RANK).astype(BF16)
        kvf = _dot(ckv, wkv_ref[...])
        kr = z[:, COL_KR:COL_KRS]
        k_rot = kr * (cos * gk) + z[:, COL_KRS:COL_NQ] * (sin * gks_ref[...])
        kr_sq = jnp.sum(kr * kr, axis=-1, keepdims=True)
        for hd in range(MLA_HEADS):
            sl = slice(hd * MLA_PAD, (hd + 1) * MLA_PAD)
            x = kvf[:, sl]
            r = _head_rsqrt(jnp.sum(x * x, axis=-1, keepdims=True) + kr_sq)
            k_ref[rows, sl] = jnp.where(shift_lane, 1.0, (x * gk + k_rot) * r).astype(BF16)
        for pr in range(MLA_HEADS // 2):
            vcols = slice(MLA_HEADS * MLA_PAD + pr * MLA_PAD, MLA_HEADS * MLA_PAD + (pr + 1) * MLA_PAD)
            v_ref[rows, 2 * pr * MLA_PAD:(2 * pr + 1) * MLA_PAD] = kvf[:, vcols].astype(BF16)
            v_ref[rows, (2 * pr + 1) * MLA_PAD:(2 * pr + 2) * MLA_PAD] = ones

        bd = bd_ref[...]
        nq = z[:, COL_NQ:COL_NK]
        nq_ref[rows, :] = (nq * lax.rsqrt(_group_mean_square(nq, bd) + EPS) * (gnq_ref[...] * NA_SCALE)).astype(BF16)
        nk = z[:, COL_NK:COL_NV]
        nk_ref[rows, :] = (nk * lax.rsqrt(_group_mean_square(nk, bd) + EPS) * gnk_ref[...]).astype(BF16)
        nv_ref[rows, :] = z[:, COL_NV:COL_U].astype(BF16)
        u_ref[rows, :] = z[:, COL_U:IN_EXT]


def _pre(x, mod, js, lw, tables, seq_len):
    T, D = x.shape
    nb = mod[2]
    tm = min(MIX_IN_ROWS, seq_len)
    per = (T // nb) // tm
    per_seq = seq_len // tm
    tok = lambda w: pl.BlockSpec((tm, w), lambda i: (i, 0))
    full = _resident
    pos = pl.BlockSpec((tm, LANES), lambda i: (i % per_seq, 0))
    weights = (lw["w_in"], lw["g_cq"], lw["w_q"], lw["g_ckv"], lw["w_kv"],
               lw["g_q"], lw["g_q_partner"], lw["g_k"], lw["g_k_partner"])
    tail = (lw["bd"], lw["g_nq"], lw["g_nk"])
    hq = MLA_HEADS * MLA_PAD
    return pl.pallas_call(
        functools.partial(_pre_kernel, sub_rows=min(MIX_IN_SUB_ROWS, tm)),
        grid=(T // tm,),
        in_specs=[tok(D)] + [_mod_spec(mod, j, per) for j in js]
                 + [full(a) for a in weights] + [pos, pos] + [full(a) for a in tail],
        out_specs=[tok(hq), tok(hq), tok(hq), tok(NA_WIDTH), tok(NA_WIDTH), tok(NA_WIDTH), tok(POOL_WIDTH)],
        out_shape=[jax.ShapeDtypeStruct((T, hq), BF16), jax.ShapeDtypeStruct((T, hq), BF16),
                   jax.ShapeDtypeStruct((T, hq), BF16), jax.ShapeDtypeStruct((T, NA_WIDTH), BF16),
                   jax.ShapeDtypeStruct((T, NA_WIDTH), BF16), jax.ShapeDtypeStruct((T, NA_WIDTH), BF16),
                   jax.ShapeDtypeStruct((T, POOL_WIDTH), F32)],
        compiler_params=_cparams(("arbitrary",), 56),
        name="mix_in",
    )(x, mod[0], mod[0], *weights, *tables, *tail)


def _flash_fast_kernel(q_ref, *refs):
    o_ref = refs[-1]
    tq = q_ref.shape[1]
    for hd in range(2):
        lanes = slice(hd * MLA_PAD, (hd + 1) * MLA_PAD)
        q = q_ref[0, :, lanes]
        acc = jnp.zeros((tq, 2 * MLA_PAD), F32)
        for k_ref, v_ref in zip(refs[0:-1:2], refs[1:-1:2]):
            for t in range(k_ref.shape[1] // MLA_KEY_TILE):
                keys = slice(t * MLA_KEY_TILE, (t + 1) * MLA_KEY_TILE)
                p = jnp.exp2(_dot_t(q, k_ref[0, keys, lanes])).astype(BF16)
                acc = acc + _dot(p, v_ref[0, keys, :])
        own = slice(hd * MLA_V, (hd + 1) * MLA_V)
        o_ref[0, :, own] = (acc[:, own] / acc[:, MLA_PAD:MLA_PAD + 1]).astype(o_ref.dtype)


def _flash_fast(q, kv):
    B, Nq, HQ = q.shape
    pairs = HQ // (2 * MLA_PAD)
    tq = min(1024, Nq)
    kv_specs = [pl.BlockSpec((1, a.shape[1], 2 * MLA_PAD), lambda b, h, i: (b, 0, h)) for pair in kv for a in pair]
    return pl.pallas_call(
        _flash_fast_kernel,
        grid=(B, pairs, Nq // tq),
        in_specs=[pl.BlockSpec((1, tq, 2 * MLA_PAD), lambda b, h, i: (b, i, h))] + kv_specs,
        out_specs=pl.BlockSpec((1, tq, 2 * MLA_V), lambda b, h, i: (b, i, h)),
        out_shape=jax.ShapeDtypeStruct((B, Nq, pairs * 2 * MLA_V), BF16),
        compiler_params=_cparams(("arbitrary", "arbitrary", "arbitrary"), 56),
        name="mla_attention",
    )(q, *[a for pair in kv for a in pair])


def _flash_safe_kernel(q_ref, k_ref, v_ref, o_ref, *, tk):
    tq = q_ref.shape[1]
    nk = k_ref.shape[1] // tk
    for hd in range(2):
        lanes = slice(hd * MLA_PAD, (hd + 1) * MLA_PAD)
        q = q_ref[0, :, lanes]

        def body(j, carry):
            m, acc = carry
            start = pl.multiple_of(j * tk, tk)
            s = _dot_t(q, k_ref[0, pl.ds(start, tk), lanes])
            m_new = jnp.maximum(m, jnp.max(s, axis=-1, keepdims=True))
            p = jnp.exp2(s - m_new)
            acc = jnp.exp2(m - m_new) * acc + _dot(p.astype(BF16), v_ref[0, pl.ds(start, tk), :])
            return m_new, acc

        init = (jnp.full((tq, 1), NEG_BIG, F32), jnp.zeros((tq, 2 * MLA_PAD), F32))
        _, acc = lax.fori_loop(0, nk, body, init)
        own = slice(hd * MLA_V, (hd + 1) * MLA_V)
        o_ref[0, :, own] = (acc[:, own] / acc[:, MLA_PAD:MLA_PAD + 1]).astype(o_ref.dtype)


def _flash_safe(q, kv):
    k = jnp.concatenate([pair[0] for pair in kv], axis=1)
    v = jnp.concatenate([pair[1] for pair in kv], axis=1)
    B, Nq, HQ = q.shape
    Nk = k.shape[1]
    pairs = HQ // (2 * MLA_PAD)
    tq = min(512, Nq)
    tk = next(t for t in (768, 512, 256) if Nk % t == 0)
    kv_spec = pl.BlockSpec((1, Nk, 2 * MLA_PAD), lambda b, h, i: (b, 0, h))
    return pl.pallas_call(
        functools.partial(_flash_safe_kernel, tk=tk),
        grid=(B, pairs, Nq // tq),
        in_specs=[pl.BlockSpec((1, tq, 2 * MLA_PAD), lambda b, h, i: (b, i, h)), kv_spec, kv_spec],
        out_specs=pl.BlockSpec((1, tq, 2 * MLA_V), lambda b, h, i: (b, i, h)),
        out_shape=jax.ShapeDtypeStruct((B, Nq, pairs * 2 * MLA_V), BF16),
        compiler_params=_cparams(("arbitrary", "arbitrary", "arbitrary"), 56),
        name="mla_attention_running_max",
    )(q, k, v)


def _mla_attention(q, kv, bound):
    return lax.cond(bound <= MLA_FAST_BOUND, lambda: _flash_fast(q, kv), lambda: _flash_safe(q, kv))


def _head_masks(width, head_dim, heads):
    lane = lax.broadcasted_iota(jnp.int32, (1, width), 1)
    return [(lane >= h * head_dim) & (lane < (h + 1) * head_dim) for h in range(heads)]


def _na_kernel(q_ref, k_ref, v_ref, kc_ref, vc_ref, bias_ref, o_ref):
    g = pl.program_id(1)
    rows = k_ref.shape[1] // GRID_W
    base = jnp.clip(g * NA_GROUP_ROWS - NA_WIN_R // 2, 0, rows - NA_KEY_ROWS)
    start = pl.multiple_of(base * GRID_W, GRID_W)
    kw = k_ref[0, pl.ds(start, NA_KEY_ROWS * GRID_W), :]
    vw = v_ref[0, pl.ds(start, NA_KEY_ROWS * GRID_W), :]
    kc, vc = kc_ref[0], vc_ref[0]
    q = q_ref[0]
    out = jnp.zeros(q.shape, F32)
    for hd, mask in enumerate(_head_masks(NA_WIDTH, NA_HEAD_DIM, NA_HEADS)):
        qh = jnp.where(mask, q, jnp.zeros_like(q))
        s_loc = _dot_t(qh, kw) + bias_ref[0, hd]
        s_ctx = _dot_t(qh, kc)
        m = jnp.maximum(jnp.max(s_loc, axis=-1, keepdims=True), jnp.max(s_ctx, axis=-1, keepdims=True))
        p_loc = jnp.exp(s_loc - m)
        p_ctx = jnp.exp(s_ctx - m)
        l = jnp.sum(p_loc, axis=-1, keepdims=True) + jnp.sum(p_ctx, axis=-1, keepdims=True)
        o = _dot(p_loc.astype(BF16), vw) + _dot(p_ctx.astype(BF16), vc)
        out = jnp.where(mask, o / l, out)
    o_ref[0] = out.astype(o_ref.dtype)


def _na(q, k, v, kc, vc, bias):
    B, S, W = q.shape
    C = kc.shape[1]
    tq = NA_GROUP_ROWS * GRID_W
    groups = S // tq

    def variant(b, g):
        return (jnp.where(g == 0, 0, jnp.where(g == groups - 1, 2, 1)), 0, 0, 0)

    return pl.pallas_call(
        _na_kernel,
        grid=(B, groups),
        in_specs=[pl.BlockSpec((1, tq, W), lambda b, g: (b, g, 0)),
                  pl.BlockSpec((1, S, W), lambda b, g: (b, 0, 0)),
                  pl.BlockSpec((1, S, W), lambda b, g: (b, 0, 0)),
                  pl.BlockSpec((1, C, W), lambda b, g: (b, 0, 0)),
                  pl.BlockSpec((1, C, W), lambda b, g: (b, 0, 0)),
                  pl.BlockSpec((1,) + bias.shape[1:], variant)],
        out_specs=pl.BlockSpec((1, tq, W), lambda b, g: (b, g, 0)),
        out_shape=jax.ShapeDtypeStruct((B, S, W), BF16),
        compiler_params=_cparams(("arbitrary", "arbitrary"), 56),
        name="na_attention",
    )(q, k, v, kc, vc, bias)


def _na_bias_tables(rpb):
    n_c = 2 * NA_WIN_C - 1
    w = np.arange(GRID_W)[:, None]
    kcol = np.arange(GRID_W)[None, :]
    cs = np.clip(w - NA_WIN_C // 2, 0, GRID_W - NA_WIN_C)
    col_ok = (kcol >= cs) & (kcol < cs + NA_WIN_C)
    col_sel = (col_ok[..., None] & ((kcol - w + NA_WIN_C - 1)[..., None] == np.arange(n_c))).astype(np.float32)
    blocks = jnp.einsum("hrc,wkc->hrwk", rpb, col_sel, precision=lax.Precision.HIGHEST)
    blocks = jnp.where(col_ok, blocks, NEG_BIG)
    masked = jnp.full((rpb.shape[0], GRID_W, GRID_W), NEG_BIG, F32)
    half = NA_WIN_R // 2
    variants = []
    for v in range(3):
        group = []
        for i in range(NA_GROUP_ROWS):
            q_rel, r0 = ((i, 0), (half + i, i), (NA_WIN_R + i, half))[v]
            row = [blocks[:, j - q_rel + NA_WIN_R - 1] if r0 <= j < r0 + NA_WIN_R else masked
                   for j in range(NA_KEY_ROWS)]
            group.append(jnp.concatenate(row, axis=-1))
        variants.append(jnp.concatenate(group, axis=1))
    return jnp.stack(variants).astype(F32)


def _small_attn_kernel(q_ref, k_ref, v_ref, o_ref):
    q, k, v = q_ref[0], k_ref[0], v_ref[0]
    out = jnp.zeros(q.shape, F32)
    for mask in _head_masks(NA_WIDTH, NA_HEAD_DIM, NA_HEADS):
        s = _dot_t(jnp.where(mask, q, jnp.zeros_like(q)), k)
        p = jnp.exp(s - jnp.max(s, axis=-1, keepdims=True))
        o = _dot(p.astype(BF16), v)
        out = jnp.where(mask, o / jnp.sum(p, axis=-1, keepdims=True), out)
    o_ref[0] = out.astype(o_ref.dtype)


def _small_attn(q, k, v):
    B, N, W = q.shape
    spec = pl.BlockSpec((1, N, W), lambda b: (b, 0, 0))
    return pl.pallas_call(
        _small_attn_kernel, grid=(B,), in_specs=[spec, spec, spec], out_specs=spec,
        out_shape=jax.ShapeDtypeStruct((B, N, W), BF16),
        compiler_params=_cparams(("arbitrary",), 32),
        name="ctx_na_attention",
    )(q, k, v)


def _mix_out_kernel(x_ref, gt_ref, ya_ref, yb_ref, u_ref, up_ref, un_ref, wo_ref, wp_ref, ps_ref, o_ref,
                    *, per_seq, seq_len):
    tm = x_ref.shape[0]
    i = pl.program_id(0) % per_seq
    u = u_ref[...]
    prev = jnp.where(i == 0, 0.0, up_ref[...])
    nxt = jnp.where(i == per_seq - 1, 0.0, un_ref[...])
    ext = jnp.concatenate([prev, u, nxt], axis=0)
    n = tm + 2 * POOL_HALO
    sums = []
    p = ext + pltpu.roll(ext, 1, 0)
    sums.append(p)
    for step in (1, 2, 4):
        p = pltpu.roll(p, step, 0) + pltpu.roll(p, n - step, 0)
        sums.append(p)
    sums = [s[POOL_HALO:POOL_HALO + tm] for s in sums]
    lane = lax.broadcasted_iota(jnp.int32, (1, POOL_WIDTH), 1)
    grp = lane // POOL_GROUP_DIM
    win = jnp.where(grp == 0, sums[0], jnp.where(grp == 1, sums[1], jnp.where(grp == 2, sums[2], sums[3])))
    half = jnp.where(grp == 0, POOL_WINDOWS[0] // 2,
                     jnp.where(grp == 1, POOL_WINDOWS[1] // 2,
                               jnp.where(grp == 2, POOL_WINDOWS[2] // 2, POOL_WINDOWS[3] // 2)))
    t = i * tm + lax.broadcasted_iota(jnp.int32, (tm, 1), 0)
    cnt = jnp.minimum(t + half, seq_len) - jnp.maximum(t - half, 0)
    pooled = win / cnt.astype(F32) - u
    yc = _dot(pooled.astype(BF16), wp_ref[...]) * ps_ref[...]
    wa = wo_ref[0:MLA_WIDTH, :]
    wb = wo_ref[MLA_WIDTH:MLA_WIDTH + NA_WIDTH, :]
    wc = wo_ref[MLA_WIDTH + NA_WIDTH:, :]
    y = _dot(ya_ref[...], wa) + _dot(yb_ref[...], wb) + _dot(yc.astype(BF16), wc)
    o_ref[...] = x_ref[...] + gt_ref[0] * y


def _mix_out(x, mod, j_gate, ya, yb, u, lw, seq_len):
    T, D = x.shape
    nb = mod[2]
    tm = min(512, seq_len)
    per = (T // nb) // tm
    per_seq = seq_len // tm
    hb = tm // POOL_HALO
    last_hb = T // POOL_HALO - 1
    tok = lambda w: pl.BlockSpec((tm, w), lambda i: (i, 0))
    full = _resident
    return pl.pallas_call(
        functools.partial(_mix_out_kernel, per_seq=per_seq, seq_len=seq_len),
        grid=(T // tm,),
        in_specs=[tok(D), _mod_spec(mod, j_gate, per),
                  tok(MLA_WIDTH), tok(NA_WIDTH), tok(POOL_WIDTH),
                  pl.BlockSpec((POOL_HALO, POOL_WIDTH), lambda i: (jnp.maximum(i * hb - 1, 0), 0)),
                  pl.BlockSpec((POOL_HALO, POOL_WIDTH), lambda i: (jnp.minimum((i + 1) * hb, last_hb), 0)),
                  full(lw["w_out"]), full(lw["w_pool"]), full(lw["pool_scale"])],
        out_specs=tok(D),
        out_shape=jax.ShapeDtypeStruct((T, D), F32),
        compiler_params=_cparams(("arbitrary",), 48),
        name="mix_out",
    )(x, mod[0], ya, yb, u, u, u, lw["w_out"], lw["w_pool"], lw["pool_scale"])


def _rope_partner(a):
    lead = a.shape[:-1]
    return a.reshape(lead + (2, 2, MLA_ROPE // 4))[..., ::-1, :].reshape(lead + (MLA_ROPE,))


def _rope_tables(n_tok):
    t = np.arange(n_tok)
    axis_dim = MLA_ROPE // 2
    inv_freq = ROPE_BASE ** (-np.arange(0, axis_dim, 2, dtype=np.float64) / axis_dim)
    ar = (t // GRID_W)[:, None] * inv_freq
    ac = (t % GRID_W)[:, None] * inv_freq
    ones = np.ones((n_tok, MLA_NOPE))
    zeros = np.zeros((n_tok, MLA_NOPE))
    pad = np.zeros((n_tok, MLA_PAD - MLA_QK))
    cos = np.concatenate([ones, np.cos(ar), np.cos(ar), np.cos(ac), np.cos(ac), pad + 1.0], axis=1)
    sin = np.concatenate([zeros, -np.sin(ar), np.sin(ar), -np.sin(ac), np.sin(ac), pad], axis=1)
    return jnp.asarray(cos, F32), jnp.asarray(sin, F32)


def _identity_tables(n_tok):
    return jnp.ones((n_tok, MLA_PAD), F32), jnp.zeros((n_tok, MLA_PAD), F32)


def _layer_weights(l, w_in, g_cq, w_q_up, g_ckv, w_kv_up, g_mla_q, g_mla_k, g_na_q, g_na_k,
                   w_pool, pool_scale, w_out):
    D = w_in.shape[1]
    wi = w_in[l]
    w_kr = wi[:, COL_KR:COL_KR + MLA_ROPE]
    in_rope_lanes = lambda w: jnp.pad(w, ((0, 0), (MLA_NOPE, MLA_PAD - MLA_QK)))
    w_in_ext = jnp.concatenate([wi[:, :COL_KR], in_rope_lanes(w_kr), in_rope_lanes(_rope_partner(w_kr)),
                                wi[:, COL_KR + MLA_ROPE:]], axis=1).astype(BF16)
    wq = w_q_up[l].reshape(MLA_Q_RANK, MLA_HEADS, MLA_QK)
    wq = jnp.concatenate([wq, _rope_partner(wq[:, :, MLA_NOPE:])], axis=2).reshape(MLA_Q_RANK, MLA_HEADS * MLA_PAD)
    wkv = w_kv_up[l].reshape(MLA_KV_RANK, MLA_HEADS, MLA_NOPE + MLA_V)
    wk = jnp.pad(wkv[:, :, :MLA_NOPE], ((0, 0), (0, 0), (0, MLA_PAD - MLA_NOPE)))
    wv = wkv[:, :, MLA_NOPE:]
    w_kv = jnp.concatenate([wk.reshape(MLA_KV_RANK, -1), wv.reshape(MLA_KV_RANK, -1)], axis=1)
    pad_gain = lambda g: jnp.pad(g, (0, MLA_PAD - MLA_QK)).reshape(1, MLA_PAD)
    partner_gain = lambda g: jnp.pad(_rope_partner(g[MLA_NOPE:]), (MLA_NOPE, MLA_PAD - MLA_QK)).reshape(1, MLA_PAD)
    head_id = np.arange(NA_WIDTH) // NA_HEAD_DIM
    bd = jnp.asarray((head_id[:, None] == head_id[None, :]) / NA_HEAD_DIM, BF16)
    wp = jnp.zeros((POOL_WIDTH, POOL_WIDTH), F32)
    for g in range(len(POOL_WINDOWS)):
        sl = slice(g * POOL_GROUP_DIM, (g + 1) * POOL_GROUP_DIM)
        wp = wp.at[sl, sl].set(w_pool[l, g])
    return {
        "w_in": w_in_ext, "g_cq": g_cq[l].reshape(1, -1), "w_q": wq.astype(BF16),
        "g_ckv": g_ckv[l].reshape(1, -1), "w_kv": w_kv.astype(BF16),
        "g_q": pad_gain(g_mla_q[l]), "g_k": pad_gain(g_mla_k[l]),
        "g_q_partner": partner_gain(g_mla_q[l]), "g_k_partner": partner_gain(g_mla_k[l]),
        "bd": bd, "g_nq": jnp.tile(g_na_q[l], NA_HEADS).reshape(1, -1), "g_nk": jnp.tile(g_na_k[l], NA_HEADS).reshape(1, -1),
        "w_pool": wp.astype(BF16), "pool_scale": pool_scale[l].reshape(1, -1), "w_out": w_out[l].astype(BF16),
    }


def kernel(x, c, ctx, c_ctx, w_mod, b_mod, w_ffn_gate, w_ffn_up, w_ffn_down, w_in, g_cq, w_q_up, g_ckv, w_kv_up,
           g_mla_q, g_mla_k, g_na_q, g_na_k, na_rpb, w_pool, pool_scale, w_out):
    B, S, D = x.shape
    C = ctx.shape[1]
    L = w_mod.shape[0]
    assert S % (NA_GROUP_ROWS * GRID_W) == 0 and S // GRID_W >= NA_KEY_ROWS and B + 1 <= SUBLANES

    cc = jnp.zeros((SUBLANES, D), F32).at[:B].set(c).at[B].set(c_ctx)
    mods = _modulation(cc, w_mod, b_mod).reshape(L * SUBLANES * N_MOD, 1, D)
    rope = _rope_tables(S)
    no_rope = _identity_tables(C)
    ffn_w = (w_ffn_gate.astype(BF16), w_ffn_up.astype(BF16), w_ffn_down.astype(BF16))

    xs = x.reshape(B * S, D)
    xc = ctx.reshape(B * C, D)
    for l in range(L):
        last = l == L - 1
        mx = (mods, l * SUBLANES * N_MOD, B)
        mc = (mods, (l * SUBLANES + B) * N_MOD, 1)
        lw = _layer_weights(l, w_in, g_cq, w_q_up, g_ckv, w_kv_up, g_mla_q, g_mla_k, g_na_q, g_na_k,
                            w_pool, pool_scale, w_out)
        xs = _ffn(xs, mx, (0, 1, 2), ffn_w, l, 0)
        xc = _ffn(xc, mc, (0, 1, 2), ffn_w, l, 0)

        qx, kx, vx, nqx, nkx, nvx, ux = _pre(xs, mx, (3, 4), lw, rope, S)
        qc, kc, vc, nqc, nkc, nvc, uc = _pre(xc, mc, (3, 4), lw, no_rope, C)
        per_b = lambda a, n: a.reshape(B, n, a.shape[-1])
        bound = _mla_score_bound(lw["g_q"], lw["g_k"])[0, 0]
        kv_c = (per_b(kc, C), per_b(vc, C))
        ya = _mla_attention(per_b(qx, S), [kv_c, (per_b(kx, S), per_b(vx, S))], bound)
        yb = _na(per_b(nqx, S), per_b(nkx, S), per_b(nvx, S), per_b(nkc, C), per_b(nvc, C),
                 _na_bias_tables(na_rpb[l]))
        xs = _mix_out(xs, mx, 5, ya.reshape(B * S, -1), yb.reshape(B * S, -1), ux, lw, S)
        xs = _ffn(xs, mx, (6, 7, 8), ffn_w, l, 1)
        if not last:
            ya_c = _mla_attention(per_b(qc, C), [kv_c], bound)
            yb_c = _small_attn(per_b(nqc, C), per_b(nkc, C), per_b(nvc, C))
            xc = _mix_out(xc, mc, 5, ya_c.reshape(B * C, -1), yb_c.reshape(B * C, -1), uc, lw, C)
            xc = _ffn(xc, mc, (6, 7, 8), ffn_w, l, 1)
    return xs.reshape(B, S, D)
```

```python
import functools
import math

import numpy as np
import jax
import jax.numpy as jnp
from jax import lax
from jax.experimental import pallas as pl
from jax.experimental.pallas import tpu as pltpu

F32 = jnp.float32
BF16 = jnp.bfloat16

LANES = 128
SUBLANES = 8

GRID_W = 64
MLA_HEADS = 8
MLA_NOPE = 64
MLA_ROPE = 32
MLA_QK = MLA_NOPE + MLA_ROPE
MLA_V = 64
MLA_Q_RANK = 768
MLA_KV_RANK = 256
MLA_WIDTH = MLA_HEADS * MLA_V
MLA_SCALE = 1.0 / math.sqrt(MLA_QK)
MLA_PAD = LANES
LOG2E = math.log2(math.e)
MLA_SCALE_LOG2 = MLA_SCALE * LOG2E
MLA_SHIFT_LANE = MLA_QK
MLA_FAST_BOUND = 60.0
MLA_KEY_TILE = 256
NA_HEADS = 4
NA_HEAD_DIM = 64
NA_WIDTH = NA_HEADS * NA_HEAD_DIM
NA_WIN_R = 8
NA_WIN_C = 16
NA_SCALE = 1.0 / math.sqrt(NA_HEAD_DIM)
NA_FAST_SPAN = 120.0
NA_GROUPS_PER_STEP = 4
NA_GROUP_ROWS = 4
NA_KEY_ROWS = NA_GROUP_ROWS + NA_WIN_R
POOL_WINDOWS = (2, 4, 8, 16)
POOL_GROUP_DIM = 64
POOL_WIDTH = len(POOL_WINDOWS) * POOL_GROUP_DIM
POOL_HALO = max(POOL_WINDOWS) // 2
FFN_CHUNK = 256
MIX_IN_ROWS = 1024
MIX_IN_SUB_ROWS = 256
MIX_OUT_ROWS = 1024
MIX_OUT_SUB_ROWS = 256
N_MOD = 9
ROPE_BASE = 10000.0
EPS = 1e-6
NEG_BIG = -1e30

COL_CQ = 0
COL_CKV = COL_CQ + MLA_Q_RANK
COL_KR = COL_CKV + MLA_KV_RANK
COL_KRS = COL_KR + MLA_PAD
COL_NQ = COL_KRS + MLA_PAD
COL_NK = COL_NQ + NA_WIDTH
COL_NV = COL_NK + NA_WIDTH
COL_U = COL_NV + NA_WIDTH
IN_EXT = COL_U + POOL_WIDTH


def _dot(a, b):
    return jnp.dot(a, b, preferred_element_type=F32)


def _dot_t(a, b):
    return lax.dot_general(a, b, (((1,), (1,)), ((), ())), preferred_element_type=F32)


def _silu(x):
    return x * jax.nn.sigmoid(x)


def _cparams(sem, vmem_mb):
    return pltpu.CompilerParams(dimension_semantics=sem, vmem_limit_bytes=vmem_mb * 1024 * 1024)


def _mod_kernel(c_ref, w_ref, b_ref, o_ref):
    h = _silu(c_ref[...]).astype(BF16)
    o_ref[0] = _dot(h, w_ref[0].astype(BF16)) + b_ref[0]


def _modulation(cc, w_mod, b_mod):
    L, D, NM = w_mod.shape
    tn = 1024
    return pl.pallas_call(
        _mod_kernel,
        grid=(L, NM // tn),
        in_specs=[pl.BlockSpec((SUBLANES, D), lambda l, j: (0, 0)),
                  pl.BlockSpec((1, D, tn), lambda l, j: (l, 0, j)),
                  pl.BlockSpec((1, 1, tn), lambda l, j: (l, 0, j))],
        out_specs=pl.BlockSpec((1, SUBLANES, tn), lambda l, j: (l, 0, j)),
        out_shape=jax.ShapeDtypeStruct((L, SUBLANES, NM), F32),
        compiler_params=_cparams(("arbitrary", "arbitrary"), 32),
        name="modulation",
    )(cc, w_mod, b_mod.reshape(L, 1, NM))


def _modulated_norm(x, shift, scale):
    ms = jnp.mean(x * x, axis=-1, keepdims=True)
    return (x * lax.rsqrt(ms + EPS)) * (1.0 + scale) + shift


def _ffn_kernel(x_ref, sh_ref, sc_ref, gt_ref, wg_ref, wu_ref, wd_ref, o_ref):
    x = x_ref[...]
    h = _modulated_norm(x, sh_ref[0], sc_ref[0]).astype(BF16)
    acc = None
    for c in range(wg_ref.shape[1] // FFN_CHUNK):
        cols = slice(c * FFN_CHUNK, (c + 1) * FFN_CHUNK)
        g = _dot(h, wg_ref[:, cols])
        u = _dot(h, wu_ref[:, cols])
        y = _dot((_silu(g) * u).astype(BF16), wd_ref[cols, :])
        acc = y if acc is None else acc + y
    o_ref[...] = x + (0.5 * gt_ref[0]) * acc


def _resident(a):
    return pl.BlockSpec(a.shape, lambda i: (0,) * a.ndim, pipeline_mode=pl.Buffered(1))


def _mod_spec(mod, j, per):
    table, first, _ = mod
    return pl.BlockSpec((1, 1, table.shape[-1]), lambda i: (first + (i // per) * N_MOD + j, 0, 0))


def _ffn(x, mod, js, weights, layer, half):
    T, D = x.shape
    nb = mod[2]
    tm = min(512, T // nb)
    per = (T // nb) // tm
    whole = lambda a: pl.BlockSpec((None, None) + a.shape[2:], lambda i: (layer, half, 0, 0),
                                   pipeline_mode=pl.Buffered(1))
    return pl.pallas_call(
        _ffn_kernel,
        grid=(T // tm,),
        in_specs=[pl.BlockSpec((tm, D), lambda i: (i, 0))] + [_mod_spec(mod, j, per) for j in js]
                 + [whole(a) for a in weights],
        out_specs=pl.BlockSpec((tm, D), lambda i: (i, 0)),
        out_shape=jax.ShapeDtypeStruct((T, D), F32),
        compiler_params=_cparams(("arbitrary",), 48),
        name="ffn",
    )(x, mod[0], mod[0], mod[0], *weights)


def _gain_norm(x, gain, width):
    ms = jnp.sum(x * x, axis=-1, keepdims=True) * (1.0 / width)
    return x * lax.rsqrt(ms + EPS) * gain


def _head_rsqrt(sumsq):
    return lax.rsqrt(sumsq * (1.0 / MLA_QK) + EPS)


def _mla_score_bound(gq, gk):
    amax = lambda g: jnp.max(jnp.abs(g), axis=-1, keepdims=True)
    return (MLA_SCALE_LOG2 * MLA_QK) * amax(gq) * amax(gk)


def _group_mean_square(x, bd):
    xx = x * x
    hi = xx.astype(BF16)
    lo = (xx - hi.astype(F32)).astype(BF16)
    return _dot(hi, bd) + _dot(lo, bd)


def _pre_kernel(x_ref, sh_ref, sc_ref, win_ref, gcq_ref, wq_ref, gckv_ref, wkv_ref, gq_ref, gqs_ref, gk_ref, gks_ref,
                cos_ref, sin_ref, bd_ref, gnq_ref, gnk_ref,
                q_ref, k_ref, v_ref, nq_ref, nk_ref, nv_ref, u_ref, *, sub_rows):
    lane = lax.broadcasted_iota(jnp.int32, (1, MLA_PAD), 1)
    real = lane < MLA_QK
    shift_lane = lane == MLA_SHIFT_LANE
    gq, gk = gq_ref[...], gk_ref[...]
    neg_bound = -_mla_score_bound(gq, gk)
    ones = jnp.ones((sub_rows, MLA_PAD), BF16)
    for s in range(x_ref.shape[0] // sub_rows):
        rows = slice(s * sub_rows, (s + 1) * sub_rows)
        h = _modulated_norm(x_ref[rows, :], sh_ref[0], sc_ref[0]).astype(BF16)
        z = _dot(h, win_ref[...])
        cos, sin = cos_ref[rows, :], sin_ref[rows, :]

        cq = _gain_norm(z[:, COL_CQ:COL_CKV], gcq_ref[...], MLA_Q_RANK).astype(BF16)
        qf = _dot(cq, wq_ref[...])
        q_cos = cos * (gq * MLA_SCALE_LOG2)
        q_sin = sin * (gqs_ref[...] * MLA_SCALE_LOG2)
        for hd in range(MLA_HEADS):
            sl = slice(hd * MLA_PAD, (hd + 1) * MLA_PAD)
            x = qf[:, sl]
            r = _head_rsqrt(jnp.sum(jnp.where(real, x * x, 0.0), axis=-1, keepdims=True))
            qh = (x * q_cos + pltpu.roll(x, MLA_PAD - MLA_ROPE, 1) * q_sin) * r
            q_ref[rows, sl] = jnp.where(shift_lane, neg_bound, qh).astype(BF16)

        ckv = _gain_norm(z[:, COL_CKV:COL_KR], gckv_ref[...], MLA_KV_RANK).astype(BF16)
        kvf = _dot(ckv, wkv_ref[...])
        kr = z[:, COL_KR:COL_KRS]
        k_rot = kr * (cos * gk) + z[:, COL_KRS:COL_NQ] * (sin * gks_ref[...])
        kr_sq = jnp.sum(kr * kr, axis=-1, keepdims=True)
        for hd in range(MLA_HEADS):
            sl = slice(hd * MLA_PAD, (hd + 1) * MLA_PAD)
            x = kvf[:, sl]
            r = _head_rsqrt(jnp.sum(x * x, axis=-1, keepdims=True) + kr_sq)
            k_ref[rows, sl] = jnp.where(shift_lane, 1.0, (x * gk + k_rot) * r).astype(BF16)
        for pr in range(MLA_HEADS // 2):
            vcols = slice(MLA_HEADS * MLA_PAD + pr * MLA_PAD, MLA_HEADS * MLA_PAD + (pr + 1) * MLA_PAD)
            v_ref[rows, 2 * pr * MLA_PAD:(2 * pr + 1) * MLA_PAD] = kvf[:, vcols].astype(BF16)
            v_ref[rows, (2 * pr + 1) * MLA_PAD:(2 * pr + 2) * MLA_PAD] = ones

        bd = bd_ref[...]
        nq = z[:, COL_NQ:COL_NK]
        nq_ref[rows, :] = (nq * lax.rsqrt(_group_mean_square(nq, bd) + EPS) * (gnq_ref[...] * (NA_SCALE * LOG2E))).astype(BF16)
        nk = z[:, COL_NK:COL_NV]
        nk_ref[rows, :] = (nk * lax.rsqrt(_group_mean_square(nk, bd) + EPS) * gnk_ref[...]).astype(BF16)
        nv_ref[rows, :] = z[:, COL_NV:COL_U].astype(BF16)
        u_ref[rows, :] = z[:, COL_U:IN_EXT]


def _pre(x, mod, js, lw, tables, seq_len):
    T, D = x.shape
    nb = mod[2]
    tm = min(MIX_IN_ROWS, seq_len)
    per = (T // nb) // tm
    per_seq = seq_len // tm
    tok = lambda w: pl.BlockSpec((tm, w), lambda i: (i, 0))
    full = _resident
    pos = pl.BlockSpec((tm, LANES), lambda i: (i % per_seq, 0))
    weights = (lw["w_in"], lw["g_cq"], lw["w_q"], lw["g_ckv"], lw["w_kv"],
               lw["g_q"], lw["g_q_partner"], lw["g_k"], lw["g_k_partner"])
    tail = (lw["bd"], lw["g_nq"], lw["g_nk"])
    hq = MLA_HEADS * MLA_PAD
    return pl.pallas_call(
        functools.partial(_pre_kernel, sub_rows=min(MIX_IN_SUB_ROWS, tm)),
        grid=(T // tm,),
        in_specs=[tok(D)] + [_mod_spec(mod, j, per) for j in js]
                 + [full(a) for a in weights] + [pos, pos] + [full(a) for a in tail],
        out_specs=[tok(hq), tok(hq), tok(hq), tok(NA_WIDTH), tok(NA_WIDTH), tok(NA_WIDTH), tok(POOL_WIDTH)],
        out_shape=[jax.ShapeDtypeStruct((T, hq), BF16), jax.ShapeDtypeStruct((T, hq), BF16),
                   jax.ShapeDtypeStruct((T, hq), BF16), jax.ShapeDtypeStruct((T, NA_WIDTH), BF16),
                   jax.ShapeDtypeStruct((T, NA_WIDTH), BF16), jax.ShapeDtypeStruct((T, NA_WIDTH), BF16),
                   jax.ShapeDtypeStruct((T, POOL_WIDTH), F32)],
        compiler_params=_cparams(("arbitrary",), 56),
        name="mix_in",
    )(x, mod[0], mod[0], *weights, *tables, *tail)


def _flash_fast_kernel(q_ref, *refs):
    o_ref = refs[-1]
    tq = q_ref.shape[1]
    for hd in range(2):
        lanes = slice(hd * MLA_PAD, (hd + 1) * MLA_PAD)
        q = q_ref[0, :, lanes]
        acc = jnp.zeros((tq, 2 * MLA_PAD), F32)
        for k_ref, v_ref in zip(refs[0:-1:2], refs[1:-1:2]):
            for t in range(k_ref.shape[1] // MLA_KEY_TILE):
                keys = slice(t * MLA_KEY_TILE, (t + 1) * MLA_KEY_TILE)
                p = jnp.exp2(_dot_t(q, k_ref[0, keys, lanes])).astype(BF16)
                acc = acc + _dot(p, v_ref[0, keys, :])
        own = slice(hd * MLA_V, (hd + 1) * MLA_V)
        o_ref[0, :, own] = (acc[:, own] / acc[:, MLA_PAD:MLA_PAD + 1]).astype(o_ref.dtype)


def _flash_fast(q, kv):
    B, Nq, HQ = q.shape
    pairs = HQ // (2 * MLA_PAD)
    tq = min(1024, Nq)
    kv_specs = [pl.BlockSpec((1, a.shape[1], 2 * MLA_PAD), lambda b, h, i: (b, 0, h)) for pair in kv for a in pair]
    return pl.pallas_call(
        _flash_fast_kernel,
        grid=(B, pairs, Nq // tq),
        in_specs=[pl.BlockSpec((1, tq, 2 * MLA_PAD), lambda b, h, i: (b, i, h))] + kv_specs,
        out_specs=pl.BlockSpec((1, tq, 2 * MLA_V), lambda b, h, i: (b, i, h)),
        out_shape=jax.ShapeDtypeStruct((B, Nq, pairs * 2 * MLA_V), BF16),
        compiler_params=_cparams(("arbitrary", "arbitrary", "arbitrary"), 56),
        name="mla_attention",
    )(q, *[a for pair in kv for a in pair])


def _flash_safe_kernel(q_ref, k_ref, v_ref, o_ref, *, tk):
    tq = q_ref.shape[1]
    nk = k_ref.shape[1] // tk
    for hd in range(2):
        lanes = slice(hd * MLA_PAD, (hd + 1) * MLA_PAD)
        q = q_ref[0, :, lanes]

        def body(j, carry):
            m, acc = carry
            start = pl.multiple_of(j * tk, tk)
            s = _dot_t(q, k_ref[0, pl.ds(start, tk), lanes])
            m_new = jnp.maximum(m, jnp.max(s, axis=-1, keepdims=True))
            p = jnp.exp2(s - m_new)
            acc = jnp.exp2(m - m_new) * acc + _dot(p.astype(BF16), v_ref[0, pl.ds(start, tk), :])
            return m_new, acc

        init = (jnp.full((tq, 1), NEG_BIG, F32), jnp.zeros((tq, 2 * MLA_PAD), F32))
        _, acc = lax.fori_loop(0, nk, body, init)
        own = slice(hd * MLA_V, (hd + 1) * MLA_V)
        o_ref[0, :, own] = (acc[:, own] / acc[:, MLA_PAD:MLA_PAD + 1]).astype(o_ref.dtype)


def _flash_safe(q, kv):
    k = jnp.concatenate([pair[0] for pair in kv], axis=1)
    v = jnp.concatenate([pair[1] for pair in kv], axis=1)
    B, Nq, HQ = q.shape
    Nk = k.shape[1]
    pairs = HQ // (2 * MLA_PAD)
    tq = min(512, Nq)
    tk = next(t for t in (768, 512, 256) if Nk % t == 0)
    kv_spec = pl.BlockSpec((1, Nk, 2 * MLA_PAD), lambda b, h, i: (b, 0, h))
    return pl.pallas_call(
        functools.partial(_flash_safe_kernel, tk=tk),
        grid=(B, pairs, Nq // tq),
        in_specs=[pl.BlockSpec((1, tq, 2 * MLA_PAD), lambda b, h, i: (b, i, h)), kv_spec, kv_spec],
        out_specs=pl.BlockSpec((1, tq, 2 * MLA_V), lambda b, h, i: (b, i, h)),
        out_shape=jax.ShapeDtypeStruct((B, Nq, pairs * 2 * MLA_V), BF16),
        compiler_params=_cparams(("arbitrary", "arbitrary", "arbitrary"), 56),
        name="mla_attention_running_max",
    )(q, k, v)


def _mla_attention(q, kv, bound):
    return lax.cond(bound <= MLA_FAST_BOUND, lambda: _flash_fast(q, kv), lambda: _flash_safe(q, kv))


def _head_masks(width, head_dim, heads):
    lane = lax.broadcasted_iota(jnp.int32, (1, width), 1)
    return [(lane >= h * head_dim) & (lane < (h + 1) * head_dim) for h in range(heads)]


def _na_kernel(q_ref, k_ref, v_ref, kc_ref, vc_ref, bias_ref, bound_ref, o_ref, *, running_max):
    rows = k_ref.shape[1] // GRID_W
    groups = rows // NA_GROUP_ROWS
    tq = NA_GROUP_ROWS * GRID_W
    kc, vc = kc_ref[0], vc_ref[0]
    bound = bound_ref[...]
    masks = _head_masks(NA_WIDTH, NA_HEAD_DIM, NA_HEADS)
    for gi in range(q_ref.shape[1] // tq):
        g = pl.program_id(1) * (q_ref.shape[1] // tq) + gi
        variant = jnp.where(g == 0, 0, jnp.where(g == groups - 1, 2, 1))
        base = jnp.clip(g * NA_GROUP_ROWS - NA_WIN_R // 2, 0, rows - NA_KEY_ROWS)
        start = pl.multiple_of(base * GRID_W, GRID_W)
        kw = k_ref[0, pl.ds(start, NA_KEY_ROWS * GRID_W), :]
        vw = v_ref[0, pl.ds(start, NA_KEY_ROWS * GRID_W), :]
        q = q_ref[0, gi * tq:(gi + 1) * tq, :]
        out = jnp.zeros(q.shape, F32)
        for hd, mask in enumerate(masks):
            qh = jnp.where(mask, q, jnp.zeros_like(q))
            s_loc = _dot_t(qh, kw) + bias_ref[variant, hd]
            s_ctx = _dot_t(qh, kc) - bound
            if running_max:
                m = jnp.maximum(jnp.max(s_loc, axis=-1, keepdims=True), jnp.max(s_ctx, axis=-1, keepdims=True))
                s_loc, s_ctx = s_loc - m, s_ctx - m
            p_loc = jnp.exp2(s_loc)
            p_ctx = jnp.exp2(s_ctx)
            l = jnp.sum(p_loc, axis=-1, keepdims=True) + jnp.sum(p_ctx, axis=-1, keepdims=True)
            o = _dot(p_loc.astype(BF16), vw) + _dot(p_ctx.astype(BF16), vc)
            out = jnp.where(mask, o / l, out)
        o_ref[0, gi * tq:(gi + 1) * tq, :] = out.astype(o_ref.dtype)


def _na_call(q, k, v, kc, vc, bias, bound, running_max):
    B, S, W = q.shape
    C = kc.shape[1]
    tq = NA_GROUP_ROWS * GRID_W * NA_GROUPS_PER_STEP
    return pl.pallas_call(
        functools.partial(_na_kernel, running_max=running_max),
        grid=(B, S // tq),
        in_specs=[pl.BlockSpec((1, tq, W), lambda b, g: (b, g, 0)),
                  pl.BlockSpec((1, S, W), lambda b, g: (b, 0, 0)),
                  pl.BlockSpec((1, S, W), lambda b, g: (b, 0, 0)),
                  pl.BlockSpec((1, C, W), lambda b, g: (b, 0, 0)),
                  pl.BlockSpec((1, C, W), lambda b, g: (b, 0, 0)),
                  pl.BlockSpec(bias.shape, lambda b, g: (0, 0, 0, 0), pipeline_mode=pl.Buffered(1)),
                  pl.BlockSpec((1, 1), lambda b, g: (0, 0))],
        out_specs=pl.BlockSpec((1, tq, W), lambda b, g: (b, g, 0)),
        out_shape=jax.ShapeDtypeStruct((B, S, W), BF16),
        compiler_params=_cparams(("arbitrary", "arbitrary"), 56),
        name="na_attention_running_max" if running_max else "na_attention",
    )(q, k, v, kc, vc, bias, bound.reshape(1, 1))


def _na(q, k, v, kc, vc, rpb, g_nq, g_nk):
    qk = (LOG2E * NA_SCALE * NA_HEAD_DIM) * jnp.max(jnp.abs(g_nq)) * jnp.max(jnp.abs(g_nk))
    hi = LOG2E * jnp.maximum(jnp.max(rpb), 0.0)
    lo = LOG2E * jnp.minimum(jnp.min(rpb), 0.0)
    bound = qk + hi
    bias = _na_bias_tables(rpb, bound)
    args = (q, k, v, kc, vc, bias, bound)
    return lax.cond(2.0 * qk + hi - lo <= NA_FAST_SPAN,
                    lambda: _na_call(*args, running_max=False), lambda: _na_call(*args, running_max=True))


def _na_bias_tables(rpb, bound):
    n_c = 2 * NA_WIN_C - 1
    w = np.arange(GRID_W)[:, None]
    kcol = np.arange(GRID_W)[None, :]
    cs = np.clip(w - NA_WIN_C // 2, 0, GRID_W - NA_WIN_C)
    col_ok = (kcol >= cs) & (kcol < cs + NA_WIN_C)
    col_sel = (col_ok[..., None] & ((kcol - w + NA_WIN_C - 1)[..., None] == np.arange(n_c))).astype(np.float32)
    blocks = jnp.einsum("hrc,wkc->hrwk", rpb, col_sel, precision=lax.Precision.HIGHEST)
    blocks = jnp.where(col_ok, LOG2E * blocks - bound, NEG_BIG)
    masked = jnp.full((rpb.shape[0], GRID_W, GRID_W), NEG_BIG, F32)
    half = NA_WIN_R // 2
    variants = []
    for v in range(3):
        group = []
        for i in range(NA_GROUP_ROWS):
            q_rel, r0 = ((i, 0), (half + i, i), (NA_WIN_R + i, half))[v]
            row = [blocks[:, j - q_rel + NA_WIN_R - 1] if r0 <= j < r0 + NA_WIN_R else masked
                   for j in range(NA_KEY_ROWS)]
            group.append(jnp.concatenate(row, axis=-1))
        variants.append(jnp.concatenate(group, axis=1))
    return jnp.stack(variants).astype(F32)


def _small_attn_kernel(q_ref, k_ref, v_ref, o_ref):
    q, k, v = q_ref[0], k_ref[0], v_ref[0]
    out = jnp.zeros(q.shape, F32)
    for mask in _head_masks(NA_WIDTH, NA_HEAD_DIM, NA_HEADS):
        s = _dot_t(jnp.where(mask, q, jnp.zeros_like(q)), k)
        p = jnp.exp2(s - jnp.max(s, axis=-1, keepdims=True))
        o = _dot(p.astype(BF16), v)
        out = jnp.where(mask, o / jnp.sum(p, axis=-1, keepdims=True), out)
    o_ref[0] = out.astype(o_ref.dtype)


def _small_attn(q, k, v):
    B, N, W = q.shape
    spec = pl.BlockSpec((1, N, W), lambda b: (b, 0, 0))
    return pl.pallas_call(
        _small_attn_kernel, grid=(B,), in_specs=[spec, spec, spec], out_specs=spec,
        out_shape=jax.ShapeDtypeStruct((B, N, W), BF16),
        compiler_params=_cparams(("arbitrary",), 32),
        name="ctx_na_attention",
    )(q, k, v)


def _mix_out_kernel(x_ref, gt_ref, ya_ref, yb_ref, u_ref, up_ref, un_ref, wo_ref, wp_ref, ps_ref, o_ref,
                    *, per_seq, seq_len):
    tm = x_ref.shape[0]
    i = pl.program_id(0) % per_seq
    prev = jnp.where(i == 0, 0.0, up_ref[...])
    nxt = jnp.where(i == per_seq - 1, 0.0, un_ref[...])
    ext = jnp.concatenate([prev, u_ref[...], nxt], axis=0)
    lane = lax.broadcasted_iota(jnp.int32, (1, POOL_WIDTH), 1)
    grp = lane // POOL_GROUP_DIM
    half = jnp.where(grp == 0, POOL_WINDOWS[0] // 2,
                     jnp.where(grp == 1, POOL_WINDOWS[1] // 2,
                               jnp.where(grp == 2, POOL_WINDOWS[2] // 2, POOL_WINDOWS[3] // 2)))
    wa = wo_ref[0:MLA_WIDTH, :]
    wb = wo_ref[MLA_WIDTH:MLA_WIDTH + NA_WIDTH, :]
    wc = wo_ref[MLA_WIDTH + NA_WIDTH:, :]
    sub = min(MIX_OUT_SUB_ROWS, tm)
    n = sub + 2 * POOL_HALO
    for s in range(tm // sub):
        rows = slice(s * sub, (s + 1) * sub)
        e = ext[s * sub:s * sub + n]
        p = e + pltpu.roll(e, 1, 0)
        sums = [p]
        for step in (1, 2, 4):
            p = pltpu.roll(p, step, 0) + pltpu.roll(p, n - step, 0)
            sums.append(p)
        sums = [w[POOL_HALO:POOL_HALO + sub] for w in sums]
        win = jnp.where(grp == 0, sums[0], jnp.where(grp == 1, sums[1], jnp.where(grp == 2, sums[2], sums[3])))
        t = i * tm + s * sub + lax.broadcasted_iota(jnp.int32, (sub, 1), 0)
        cnt = jnp.minimum(t + half, seq_len) - jnp.maximum(t - half, 0)
        pooled = win / cnt.astype(F32) - e[POOL_HALO:POOL_HALO + sub]
        yc = _dot(pooled.astype(BF16), wp_ref[...]) * ps_ref[...]
        y = _dot(ya_ref[rows, :], wa) + _dot(yb_ref[rows, :], wb) + _dot(yc.astype(BF16), wc)
        o_ref[rows, :] = x_ref[rows, :] + gt_ref[0] * y


def _mix_out(x, mod, j_gate, ya, yb, u, lw, seq_len):
    T, D = x.shape
    nb = mod[2]
    tm = min(MIX_OUT_ROWS, seq_len)
    per = (T // nb) // tm
    per_seq = seq_len // tm
    hb = tm // POOL_HALO
    last_hb = T // POOL_HALO - 1
    tok = lambda w: pl.BlockSpec((tm, w), lambda i: (i, 0))
    full = _resident
    return pl.pallas_call(
        functools.partial(_mix_out_kernel, per_seq=per_seq, seq_len=seq_len),
        grid=(T // tm,),
        in_specs=[tok(D), _mod_spec(mod, j_gate, per),
                  tok(MLA_WIDTH), tok(NA_WIDTH), tok(POOL_WIDTH),
                  pl.BlockSpec((POOL_HALO, POOL_WIDTH), lambda i: (jnp.maximum(i * hb - 1, 0), 0)),
                  pl.BlockSpec((POOL_HALO, POOL_WIDTH), lambda i: (jnp.minimum((i + 1) * hb, last_hb), 0)),
                  full(lw["w_out"]), full(lw["w_pool"]), full(lw["pool_scale"])],
        out_specs=tok(D),
        out_shape=jax.ShapeDtypeStruct((T, D), F32),
        compiler_params=_cparams(("arbitrary",), 48),
        name="mix_out",
    )(x, mod[0], ya, yb, u, u, u, lw["w_out"], lw["w_pool"], lw["pool_scale"])


def _rope_partner(a):
    lead = a.shape[:-1]
    return a.reshape(lead + (2, 2, MLA_ROPE // 4))[..., ::-1, :].reshape(lead + (MLA_ROPE,))


def _rope_tables(n_tok):
    t = np.arange(n_tok)
    axis_dim = MLA_ROPE // 2
    inv_freq = ROPE_BASE ** (-np.arange(0, axis_dim, 2, dtype=np.float64) / axis_dim)
    ar = (t // GRID_W)[:, None] * inv_freq
    ac = (t % GRID_W)[:, None] * inv_freq
    ones = np.ones((n_tok, MLA_NOPE))
    zeros = np.zeros((n_tok, MLA_NOPE))
    pad = np.zeros((n_tok, MLA_PAD - MLA_QK))
    cos = np.concatenate([ones, np.cos(ar), np.cos(ar), np.cos(ac), np.cos(ac), pad + 1.0], axis=1)
    sin = np.concatenate([zeros, -np.sin(ar), np.sin(ar), -np.sin(ac), np.sin(ac), pad], axis=1)
    return jnp.asarray(cos, F32), jnp.asarray(sin, F32)


def _identity_tables(n_tok):
    return jnp.ones((n_tok, MLA_PAD), F32), jnp.zeros((n_tok, MLA_PAD), F32)


def _layer_weights(l, w_in, g_cq, w_q_up, g_ckv, w_kv_up, g_mla_q, g_mla_k, g_na_q, g_na_k,
                   w_pool, pool_scale, w_out):
    D = w_in.shape[1]
    wi = w_in[l]
    w_kr = wi[:, COL_KR:COL_KR + MLA_ROPE]
    in_rope_lanes = lambda w: jnp.pad(w, ((0, 0), (MLA_NOPE, MLA_PAD - MLA_QK)))
    w_in_ext = jnp.concatenate([wi[:, :COL_KR], in_rope_lanes(w_kr), in_rope_lanes(_rope_partner(w_kr)),
                                wi[:, COL_KR + MLA_ROPE:]], axis=1).astype(BF16)
    wq = w_q_up[l].reshape(MLA_Q_RANK, MLA_HEADS, MLA_QK)
    wq = jnp.concatenate([wq, _rope_partner(wq[:, :, MLA_NOPE:])], axis=2).reshape(MLA_Q_RANK, MLA_HEADS * MLA_PAD)
    wkv = w_kv_up[l].reshape(MLA_KV_RANK, MLA_HEADS, MLA_NOPE + MLA_V)
    wk = jnp.pad(wkv[:, :, :MLA_NOPE], ((0, 0), (0, 0), (0, MLA_PAD - MLA_NOPE)))
    wv = wkv[:, :, MLA_NOPE:]
    w_kv = jnp.concatenate([wk.reshape(MLA_KV_RANK, -1), wv.reshape(MLA_KV_RANK, -1)], axis=1)
    pad_gain = lambda g: jnp.pad(g, (0, MLA_PAD - MLA_QK)).reshape(1, MLA_PAD)
    partner_gain = lambda g: jnp.pad(_rope_partner(g[MLA_NOPE:]), (MLA_NOPE, MLA_PAD - MLA_QK)).reshape(1, MLA_PAD)
    head_id = np.arange(NA_WIDTH) // NA_HEAD_DIM
    bd = jnp.asarray((head_id[:, None] == head_id[None, :]) / NA_HEAD_DIM, BF16)
    wp = jnp.zeros((POOL_WIDTH, POOL_WIDTH), F32)
    for g in range(len(POOL_WINDOWS)):
        sl = slice(g * POOL_GROUP_DIM, (g + 1) * POOL_GROUP_DIM)
        wp = wp.at[sl, sl].set(w_pool[l, g])
    return {
        "w_in": w_in_ext, "g_cq": g_cq[l].reshape(1, -1), "w_q": wq.astype(BF16),
        "g_ckv": g_ckv[l].reshape(1, -1), "w_kv": w_kv.astype(BF16),
        "g_q": pad_gain(g_mla_q[l]), "g_k": pad_gain(g_mla_k[l]),
        "g_q_partner": partner_gain(g_mla_q[l]), "g_k_partner": partner_gain(g_mla_k[l]),
        "bd": bd, "g_nq": jnp.tile(g_na_q[l], NA_HEADS).reshape(1, -1), "g_nk": jnp.tile(g_na_k[l], NA_HEADS).reshape(1, -1),
        "w_pool": wp.astype(BF16), "pool_scale": pool_scale[l].reshape(1, -1), "w_out": w_out[l].astype(BF16),
    }


def kernel(x, c, ctx, c_ctx, w_mod, b_mod, w_ffn_gate, w_ffn_up, w_ffn_down, w_in, g_cq, w_q_up, g_ckv, w_kv_up,
           g_mla_q, g_mla_k, g_na_q, g_na_k, na_rpb, w_pool, pool_scale, w_out):
    B, S, D = x.shape
    C = ctx.shape[1]
    L = w_mod.shape[0]
    assert S % (NA_GROUPS_PER_STEP * NA_GROUP_ROWS * GRID_W) == 0 and S // GRID_W >= NA_KEY_ROWS and B + 1 <= SUBLANES

    cc = jnp.zeros((SUBLANES, D), F32).at[:B].set(c).at[B].set(c_ctx)
    mods = _modulation(cc, w_mod, b_mod).reshape(L * SUBLANES * N_MOD, 1, D)
    rope = _rope_tables(S)
    no_rope = _identity_tables(C)
    ffn_w = (w_ffn_gate.astype(BF16), w_ffn_up.astype(BF16), w_ffn_down.astype(BF16))

    xs = x.reshape(B * S, D)
    xc = ctx.reshape(B * C, D)
    for l in range(L):
        last = l == L - 1
        mx = (mods, l * SUBLANES * N_MOD, B)
        mc = (mods, (l * SUBLANES + B) * N_MOD, 1)
        lw = _layer_weights(l, w_in, g_cq, w_q_up, g_ckv, w_kv_up, g_mla_q, g_mla_k, g_na_q, g_na_k,
                            w_pool, pool_scale, w_out)
        xs = _ffn(xs, mx, (0, 1, 2), ffn_w, l, 0)
        xc = _ffn(xc, mc, (0, 1, 2), ffn_w, l, 0)

        qx, kx, vx, nqx, nkx, nvx, ux = _pre(xs, mx, (3, 4), lw, rope, S)
        qc, kc, vc, nqc, nkc, nvc, uc = _pre(xc, mc, (3, 4), lw, no_rope, C)
        per_b = lambda a, n: a.reshape(B, n, a.shape[-1])
        bound = _mla_score_bound(lw["g_q"], lw["g_k"])[0, 0]
        kv_c = (per_b(kc, C), per_b(vc, C))
        ya = _mla_attention(per_b(qx, S), [kv_c, (per_b(kx, S), per_b(vx, S))], bound)
        yb = _na(per_b(nqx, S), per_b(nkx, S), per_b(nvx, S), per_b(nkc, C), per_b(nvc, C),
                 na_rpb[l], g_na_q[l], g_na_k[l])
        xs = _mix_out(xs, mx, 5, ya.reshape(B * S, -1), yb.reshape(B * S, -1), ux, lw, S)
        xs = _ffn(xs, mx, (6, 7, 8), ffn_w, l, 1)
        if not last:
            ya_c = _mla_attention(per_b(qc, C), [kv_c], bound)
            yb_c = _small_attn(per_b(nqc, C), per_b(nkc, C), per_b(nvc, C))
            xc = _mix_out(xc, mc, 5, ya_c.reshape(B * C, -1), yb_c.reshape(B * C, -1), uc, lw, C)
            xc = _ffn(xc, mc, (6, 7, 8), ffn_w, l, 1)
    return xs.reshape(B, S, D)
```

```python
import functools
import math

import numpy as np
import jax
import jax.numpy as jnp
from jax import lax
from jax.experimental import pallas as pl
from jax.experimental.pallas import tpu as pltpu

F32 = jnp.float32
BF16 = jnp.bfloat16

LANES = 128
SUBLANES = 8

GRID_W = 64
MLA_HEADS = 8
MLA_NOPE = 64
MLA_ROPE = 32
MLA_QK = MLA_NOPE + MLA_ROPE
MLA_V = 64
MLA_Q_RANK = 768
MLA_KV_RANK = 256
MLA_WIDTH = MLA_HEADS * MLA_V
MLA_SCALE = 1.0 / math.sqrt(MLA_QK)
MLA_PAD = LANES
LOG2E = math.log2(math.e)
MLA_SCALE_LOG2 = MLA_SCALE * LOG2E
MLA_SHIFT_LANE = MLA_QK
MLA_FAST_BOUND = 60.0
MLA_KEY_TILE = 256
NA_HEADS = 4
NA_HEAD_DIM = 64
NA_WIDTH = NA_HEADS * NA_HEAD_DIM
NA_WIN_R = 8
NA_WIN_C = 16
NA_SCALE = 1.0 / math.sqrt(NA_HEAD_DIM)
NA_FAST_SPAN = 120.0
NA_GROUPS_PER_STEP = 4
NA_GROUP_ROWS = 4
NA_KEY_ROWS = NA_GROUP_ROWS + NA_WIN_R
POOL_WINDOWS = (2, 4, 8, 16)
POOL_GROUP_DIM = 64
POOL_WIDTH = len(POOL_WINDOWS) * POOL_GROUP_DIM
POOL_HALO = max(POOL_WINDOWS) // 2
FFN_CHUNK = 256
MIX_IN_ROWS = 1024
MIX_IN_SUB_ROWS = 256
MIX_OUT_ROWS = 1024
MIX_OUT_SUB_ROWS = 256
N_MOD = 9
ROPE_BASE = 10000.0
EPS = 1e-6
NEG_BIG = -1e30

COL_CQ = 0
COL_CKV = COL_CQ + MLA_Q_RANK
COL_KR = COL_CKV + MLA_KV_RANK
COL_KRS = COL_KR + MLA_PAD
COL_NQ = COL_KRS + MLA_PAD
COL_NK = COL_NQ + NA_WIDTH
COL_NV = COL_NK + NA_WIDTH
COL_U = COL_NV + NA_WIDTH
IN_EXT = COL_U + POOL_WIDTH


def _dot(a, b):
    return jnp.dot(a, b, preferred_element_type=F32)


def _dot_t(a, b):
    return lax.dot_general(a, b, (((1,), (1,)), ((), ())), preferred_element_type=F32)


def _silu(x):
    return x * jax.nn.sigmoid(x)


def _cparams(sem, vmem_mb):
    return pltpu.CompilerParams(dimension_semantics=sem, vmem_limit_bytes=vmem_mb * 1024 * 1024)


def _mod_kernel(c_ref, w_ref, b_ref, o_ref):
    h = _silu(c_ref[...]).astype(BF16)
    o_ref[0] = _dot(h, w_ref[0].astype(BF16)) + b_ref[0]


def _modulation(cc, w_mod, b_mod):
    L, D, NM = w_mod.shape
    tn = 1024
    return pl.pallas_call(
        _mod_kernel,
        grid=(L, NM // tn),
        in_specs=[pl.BlockSpec((SUBLANES, D), lambda l, j: (0, 0)),
                  pl.BlockSpec((1, D, tn), lambda l, j: (l, 0, j)),
                  pl.BlockSpec((1, 1, tn), lambda l, j: (l, 0, j))],
        out_specs=pl.BlockSpec((1, SUBLANES, tn), lambda l, j: (l, 0, j)),
        out_shape=jax.ShapeDtypeStruct((L, SUBLANES, NM), F32),
        compiler_params=_cparams(("arbitrary", "arbitrary"), 32),
        name="modulation",
    )(cc, w_mod, b_mod.reshape(L, 1, NM))


def _modulated_norm(x, shift, scale):
    ms = jnp.mean(x * x, axis=-1, keepdims=True)
    return (x * lax.rsqrt(ms + EPS)) * (1.0 + scale) + shift


def _ffn_kernel(x_ref, sh_ref, sc_ref, gt_ref, wg_ref, wu_ref, wd_ref, o_ref):
    x = x_ref[...]
    h = _modulated_norm(x, sh_ref[0], sc_ref[0]).astype(BF16)
    acc = None
    for c in range(wg_ref.shape[1] // FFN_CHUNK):
        cols = slice(c * FFN_CHUNK, (c + 1) * FFN_CHUNK)
        g = _dot(h, wg_ref[:, cols])
        u = _dot(h, wu_ref[:, cols])
        y = _dot((_silu(g) * u).astype(BF16), wd_ref[cols, :])
        acc = y if acc is None else acc + y
    o_ref[...] = x + (0.5 * gt_ref[0]) * acc


def _resident(a):
    return pl.BlockSpec(a.shape, lambda i: (0,) * a.ndim, pipeline_mode=pl.Buffered(1))


def _mod_spec(mod, j, per):
    table, first, _ = mod
    return pl.BlockSpec((1, 1, table.shape[-1]), lambda i: (first + (i // per) * N_MOD + j, 0, 0))


def _ffn(x, mod, js, weights, layer, half):
    T, D = x.shape
    nb = mod[2]
    tm = min(512, T // nb)
    per = (T // nb) // tm
    whole = lambda a: pl.BlockSpec((None, None) + a.shape[2:], lambda i: (layer, half, 0, 0),
                                   pipeline_mode=pl.Buffered(1))
    return pl.pallas_call(
        _ffn_kernel,
        grid=(T // tm,),
        in_specs=[pl.BlockSpec((tm, D), lambda i: (i, 0))] + [_mod_spec(mod, j, per) for j in js]
                 + [whole(a) for a in weights],
        out_specs=pl.BlockSpec((tm, D), lambda i: (i, 0)),
        out_shape=jax.ShapeDtypeStruct((T, D), F32),
        compiler_params=_cparams(("arbitrary",), 48),
        name="ffn",
    )(x, mod[0], mod[0], mod[0], *weights)


def _gain_norm(x, gain, width):
    ms = jnp.sum(x * x, axis=-1, keepdims=True) * (1.0 / width)
    return x * lax.rsqrt(ms + EPS) * gain


def _head_rsqrt(sumsq):
    return lax.rsqrt(sumsq * (1.0 / MLA_QK) + EPS)


def _mla_score_bound(gq, gk):
    amax = lambda g: jnp.max(jnp.abs(g), axis=-1, keepdims=True)
    return (MLA_SCALE_LOG2 * MLA_QK) * amax(gq) * amax(gk)


def _group_mean_square(x, bd):
    xx = x * x
    hi = xx.astype(BF16)
    lo = (xx - hi.astype(F32)).astype(BF16)
    return _dot(hi, bd) + _dot(lo, bd)


def _pre_kernel(x_ref, sh_ref, sc_ref, win_ref, gcq_ref, wq_ref, gckv_ref, wkv_ref, gq_ref, gqs_ref, gk_ref, gks_ref,
                cos_ref, sin_ref, bd_ref, gnq_ref, gnk_ref,
                q_ref, k_ref, v_ref, nq_ref, nk_ref, nv_ref, u_ref, *, sub_rows):
    lane = lax.broadcasted_iota(jnp.int32, (1, MLA_PAD), 1)
    real = lane < MLA_QK
    shift_lane = lane == MLA_SHIFT_LANE
    gq, gk = gq_ref[...], gk_ref[...]
    neg_bound = -_mla_score_bound(gq, gk)
    ones = jnp.ones((sub_rows, MLA_PAD), BF16)
    for s in range(x_ref.shape[0] // sub_rows):
        rows = slice(s * sub_rows, (s + 1) * sub_rows)
        h = _modulated_norm(x_ref[rows, :], sh_ref[0], sc_ref[0]).astype(BF16)
        z = _dot(h, win_ref[...])
        cos, sin = cos_ref[rows, :], sin_ref[rows, :]

        cq = _gain_norm(z[:, COL_CQ:COL_CKV], gcq_ref[...], MLA_Q_RANK).astype(BF16)
        qf = _dot(cq, wq_ref[...])
        q_cos = cos * (gq * MLA_SCALE_LOG2)
        q_sin = sin * (gqs_ref[...] * MLA_SCALE_LOG2)
        for hd in range(MLA_HEADS):
            sl = slice(hd * MLA_PAD, (hd + 1) * MLA_PAD)
            x = qf[:, sl]
            r = _head_rsqrt(jnp.sum(jnp.where(real, x * x, 0.0), axis=-1, keepdims=True))
            qh = (x * q_cos + pltpu.roll(x, MLA_PAD - MLA_ROPE, 1) * q_sin) * r
            q_ref[rows, sl] = jnp.where(shift_lane, neg_bound, qh).astype(BF16)

        ckv = _gain_norm(z[:, COL_CKV:COL_KR], gckv_ref[...], MLA_KV_RANK).astype(BF16)
        kvf = _dot(ckv, wkv_ref[...])
        kr = z[:, COL_KR:COL_KRS]
        k_rot = kr * (cos * gk) + z[:, COL_KRS:COL_NQ] * (sin * gks_ref[...])
        kr_sq = jnp.sum(kr * kr, axis=-1, keepdims=True)
        for hd in range(MLA_HEADS):
            sl = slice(hd * MLA_PAD, (hd + 1) * MLA_PAD)
            x = kvf[:, sl]
            r = _head_rsqrt(jnp.sum(x * x, axis=-1, keepdims=True) + kr_sq)
            k_ref[rows, sl] = jnp.where(shift_lane, 1.0, (x * gk + k_rot) * r).astype(BF16)
        for pr in range(MLA_HEADS // 2):
            vcols = slice(MLA_HEADS * MLA_PAD + pr * MLA_PAD, MLA_HEADS * MLA_PAD + (pr + 1) * MLA_PAD)
            v_ref[rows, 2 * pr * MLA_PAD:(2 * pr + 1) * MLA_PAD] = kvf[:, vcols].astype(BF16)
            v_ref[rows, (2 * pr + 1) * MLA_PAD:(2 * pr + 2) * MLA_PAD] = ones

        bd = bd_ref[...]
        nq = z[:, COL_NQ:COL_NK]
        nq_ref[rows, :] = (nq * lax.rsqrt(_group_mean_square(nq, bd) + EPS) * (gnq_ref[...] * (NA_SCALE * LOG2E))).astype(BF16)
        nk = z[:, COL_NK:COL_NV]
        nk_ref[rows, :] = (nk * lax.rsqrt(_group_mean_square(nk, bd) + EPS) * gnk_ref[...]).astype(BF16)
        nv_ref[rows, :] = z[:, COL_NV:COL_U].astype(BF16)
        u_ref[rows, :] = z[:, COL_U:IN_EXT]


def _pre(x, mod, js, lw, tables, seq_len):
    T, D = x.shape
    nb = mod[2]
    tm = min(MIX_IN_ROWS, seq_len)
    per = (T // nb) // tm
    per_seq = seq_len // tm
    tok = lambda w: pl.BlockSpec((tm, w), lambda i: (i, 0))
    full = _resident
    pos = pl.BlockSpec((tm, LANES), lambda i: (i % per_seq, 0))
    weights = (lw["w_in"], lw["g_cq"], lw["w_q"], lw["g_ckv"], lw["w_kv"],
               lw["g_q"], lw["g_q_partner"], lw["g_k"], lw["g_k_partner"])
    tail = (lw["bd"], lw["g_nq"], lw["g_nk"])
    hq = MLA_HEADS * MLA_PAD
    return pl.pallas_call(
        functools.partial(_pre_kernel, sub_rows=min(MIX_IN_SUB_ROWS, tm)),
        grid=(T // tm,),
        in_specs=[tok(D)] + [_mod_spec(mod, j, per) for j in js]
                 + [full(a) for a in weights] + [pos, pos] + [full(a) for a in tail],
        out_specs=[tok(hq), tok(hq), tok(hq), tok(NA_WIDTH), tok(NA_WIDTH), tok(NA_WIDTH), tok(POOL_WIDTH)],
        out_shape=[jax.ShapeDtypeStruct((T, hq), BF16), jax.ShapeDtypeStruct((T, hq), BF16),
                   jax.ShapeDtypeStruct((T, hq), BF16), jax.ShapeDtypeStruct((T, NA_WIDTH), BF16),
                   jax.ShapeDtypeStruct((T, NA_WIDTH), BF16), jax.ShapeDtypeStruct((T, NA_WIDTH), BF16),
                   jax.ShapeDtypeStruct((T, POOL_WIDTH), F32)],
        compiler_params=_cparams(("arbitrary",), 56),
        name="mix_in",
    )(x, mod[0], mod[0], *weights, *tables, *tail)


def _flash_fast_kernel(q_ref, *refs):
    o_ref = refs[-1]
    tq = q_ref.shape[1]
    lanes = [slice(hd * MLA_PAD, (hd + 1) * MLA_PAD) for hd in range(2)]
    q = [q_ref[0, :, sl] for sl in lanes]
    acc = [jnp.zeros((tq, 2 * MLA_PAD), F32) for _ in range(2)]
    for k_ref, v_ref in zip(refs[0:-1:2], refs[1:-1:2]):
        for t in range(k_ref.shape[1] // MLA_KEY_TILE):
            keys = slice(t * MLA_KEY_TILE, (t + 1) * MLA_KEY_TILE)
            for hd in range(2):
                p = jnp.exp2(_dot_t(q[hd], k_ref[0, keys, lanes[hd]])).astype(BF16)
                acc[hd] = acc[hd] + _dot(p, v_ref[0, keys, :])
    for hd in range(2):
        own = slice(hd * MLA_V, (hd + 1) * MLA_V)
        o_ref[0, :, own] = (acc[hd][:, own] / acc[hd][:, MLA_PAD:MLA_PAD + 1]).astype(o_ref.dtype)


def _flash_fast(q, kv):
    B, Nq, HQ = q.shape
    pairs = HQ // (2 * MLA_PAD)
    tq = min(1024, Nq)
    kv_specs = [pl.BlockSpec((1, a.shape[1], 2 * MLA_PAD), lambda b, h, i: (b, 0, h)) for pair in kv for a in pair]
    return pl.pallas_call(
        _flash_fast_kernel,
        grid=(B, pairs, Nq // tq),
        in_specs=[pl.BlockSpec((1, tq, 2 * MLA_PAD), lambda b, h, i: (b, i, h))] + kv_specs,
        out_specs=pl.BlockSpec((1, tq, 2 * MLA_V), lambda b, h, i: (b, i, h)),
        out_shape=jax.ShapeDtypeStruct((B, Nq, pairs * 2 * MLA_V), BF16),
        compiler_params=_cparams(("arbitrary", "arbitrary", "arbitrary"), 56),
        name="mla_attention",
    )(q, *[a for pair in kv for a in pair])


def _flash_safe_kernel(q_ref, k_ref, v_ref, o_ref, *, tk):
    tq = q_ref.shape[1]
    nk = k_ref.shape[1] // tk
    for hd in range(2):
        lanes = slice(hd * MLA_PAD, (hd + 1) * MLA_PAD)
        q = q_ref[0, :, lanes]

        def body(j, carry):
            m, acc = carry
            start = pl.multiple_of(j * tk, tk)
            s = _dot_t(q, k_ref[0, pl.ds(start, tk), lanes])
            m_new = jnp.maximum(m, jnp.max(s, axis=-1, keepdims=True))
            p = jnp.exp2(s - m_new)
            acc = jnp.exp2(m - m_new) * acc + _dot(p.astype(BF16), v_ref[0, pl.ds(start, tk), :])
            return m_new, acc

        init = (jnp.full((tq, 1), NEG_BIG, F32), jnp.zeros((tq, 2 * MLA_PAD), F32))
        _, acc = lax.fori_loop(0, nk, body, init)
        own = slice(hd * MLA_V, (hd + 1) * MLA_V)
        o_ref[0, :, own] = (acc[:, own] / acc[:, MLA_PAD:MLA_PAD + 1]).astype(o_ref.dtype)


def _flash_safe(q, kv):
    k = jnp.concatenate([pair[0] for pair in kv], axis=1)
    v = jnp.concatenate([pair[1] for pair in kv], axis=1)
    B, Nq, HQ = q.shape
    Nk = k.shape[1]
    pairs = HQ // (2 * MLA_PAD)
    tq = min(512, Nq)
    tk = next(t for t in (768, 512, 256) if Nk % t == 0)
    kv_spec = pl.BlockSpec((1, Nk, 2 * MLA_PAD), lambda b, h, i: (b, 0, h))
    return pl.pallas_call(
        functools.partial(_flash_safe_kernel, tk=tk),
        grid=(B, pairs, Nq // tq),
        in_specs=[pl.BlockSpec((1, tq, 2 * MLA_PAD), lambda b, h, i: (b, i, h)), kv_spec, kv_spec],
        out_specs=pl.BlockSpec((1, tq, 2 * MLA_V), lambda b, h, i: (b, i, h)),
        out_shape=jax.ShapeDtypeStruct((B, Nq, pairs * 2 * MLA_V), BF16),
        compiler_params=_cparams(("arbitrary", "arbitrary", "arbitrary"), 56),
        name="mla_attention_running_max",
    )(q, k, v)


def _mla_attention(q, kv, bound):
    return lax.cond(bound <= MLA_FAST_BOUND, lambda: _flash_fast(q, kv), lambda: _flash_safe(q, kv))


def _head_masks(width, head_dim, heads):
    lane = lax.broadcasted_iota(jnp.int32, (1, width), 1)
    return [(lane >= h * head_dim) & (lane < (h + 1) * head_dim) for h in range(heads)]


def _na_kernel(q_ref, k_ref, v_ref, kc_ref, vc_ref, bias_ref, bound_ref, o_ref, *, running_max):
    rows = k_ref.shape[1] // GRID_W
    groups = rows // NA_GROUP_ROWS
    tq = NA_GROUP_ROWS * GRID_W
    kc, vc = kc_ref[0], vc_ref[0]
    bound = bound_ref[...]
    masks = _head_masks(NA_WIDTH, NA_HEAD_DIM, NA_HEADS)
    for gi in range(q_ref.shape[1] // tq):
        g = pl.program_id(1) * (q_ref.shape[1] // tq) + gi
        variant = jnp.where(g == 0, 0, jnp.where(g == groups - 1, 2, 1))
        base = jnp.clip(g * NA_GROUP_ROWS - NA_WIN_R // 2, 0, rows - NA_KEY_ROWS)
        start = pl.multiple_of(base * GRID_W, GRID_W)
        kw = k_ref[0, pl.ds(start, NA_KEY_ROWS * GRID_W), :]
        vw = v_ref[0, pl.ds(start, NA_KEY_ROWS * GRID_W), :]
        q = q_ref[0, gi * tq:(gi + 1) * tq, :]
        out = jnp.zeros(q.shape, F32)
        for hd, mask in enumerate(masks):
            qh = jnp.where(mask, q, jnp.zeros_like(q))
            s_loc = _dot_t(qh, kw) + bias_ref[variant, hd]
            s_ctx = _dot_t(qh, kc) - bound
            if running_max:
                m = jnp.maximum(jnp.max(s_loc, axis=-1, keepdims=True), jnp.max(s_ctx, axis=-1, keepdims=True))
                s_loc, s_ctx = s_loc - m, s_ctx - m
            p_loc = jnp.exp2(s_loc)
            p_ctx = jnp.exp2(s_ctx)
            l = jnp.sum(p_loc, axis=-1, keepdims=True) + jnp.sum(p_ctx, axis=-1, keepdims=True)
            o = _dot(p_loc.astype(BF16), vw) + _dot(p_ctx.astype(BF16), vc)
            out = jnp.where(mask, o / l, out)
        o_ref[0, gi * tq:(gi + 1) * tq, :] = out.astype(o_ref.dtype)


def _na_call(q, k, v, kc, vc, bias, bound, running_max):
    B, S, W = q.shape
    C = kc.shape[1]
    tq = NA_GROUP_ROWS * GRID_W * NA_GROUPS_PER_STEP
    return pl.pallas_call(
        functools.partial(_na_kernel, running_max=running_max),
        grid=(B, S // tq),
        in_specs=[pl.BlockSpec((1, tq, W), lambda b, g: (b, g, 0)),
                  pl.BlockSpec((1, S, W), lambda b, g: (b, 0, 0)),
                  pl.BlockSpec((1, S, W), lambda b, g: (b, 0, 0)),
                  pl.BlockSpec((1, C, W), lambda b, g: (b, 0, 0)),
                  pl.BlockSpec((1, C, W), lambda b, g: (b, 0, 0)),
                  pl.BlockSpec(bias.shape, lambda b, g: (0, 0, 0, 0), pipeline_mode=pl.Buffered(1)),
                  pl.BlockSpec((1, 1), lambda b, g: (0, 0))],
        out_specs=pl.BlockSpec((1, tq, W), lambda b, g: (b, g, 0)),
        out_shape=jax.ShapeDtypeStruct((B, S, W), BF16),
        compiler_params=_cparams(("arbitrary", "arbitrary"), 56),
        name="na_attention_running_max" if running_max else "na_attention",
    )(q, k, v, kc, vc, bias, bound.reshape(1, 1))


def _na(q, k, v, kc, vc, rpb, g_nq, g_nk):
    qk = (LOG2E * NA_SCALE * NA_HEAD_DIM) * jnp.max(jnp.abs(g_nq)) * jnp.max(jnp.abs(g_nk))
    hi = LOG2E * jnp.maximum(jnp.max(rpb), 0.0)
    lo = LOG2E * jnp.minimum(jnp.min(rpb), 0.0)
    bound = qk + hi
    bias = _na_bias_tables(rpb, bound)
    args = (q, k, v, kc, vc, bias, bound)
    return lax.cond(2.0 * qk + hi - lo <= NA_FAST_SPAN,
                    lambda: _na_call(*args, running_max=False), lambda: _na_call(*args, running_max=True))


def _na_bias_tables(rpb, bound):
    n_c = 2 * NA_WIN_C - 1
    w = np.arange(GRID_W)[:, None]
    kcol = np.arange(GRID_W)[None, :]
    cs = np.clip(w - NA_WIN_C // 2, 0, GRID_W - NA_WIN_C)
    col_ok = (kcol >= cs) & (kcol < cs + NA_WIN_C)
    col_sel = (col_ok[..., None] & ((kcol - w + NA_WIN_C - 1)[..., None] == np.arange(n_c))).astype(np.float32)
    blocks = jnp.einsum("hrc,wkc->hrwk", rpb, col_sel, precision=lax.Precision.HIGHEST)
    blocks = jnp.where(col_ok, LOG2E * blocks - bound, NEG_BIG)
    masked = jnp.full((rpb.shape[0], GRID_W, GRID_W), NEG_BIG, F32)
    half = NA_WIN_R // 2
    variants = []
    for v in range(3):
        group = []
        for i in range(NA_GROUP_ROWS):
            q_rel, r0 = ((i, 0), (half + i, i), (NA_WIN_R + i, half))[v]
            row = [blocks[:, j - q_rel + NA_WIN_R - 1] if r0 <= j < r0 + NA_WIN_R else masked
                   for j in range(NA_KEY_ROWS)]
            group.append(jnp.concatenate(row, axis=-1))
        variants.append(jnp.concatenate(group, axis=1))
    return jnp.stack(variants).astype(F32)


def _small_attn_kernel(q_ref, k_ref, v_ref, o_ref):
    q, k, v = q_ref[0], k_ref[0], v_ref[0]
    out = jnp.zeros(q.shape, F32)
    for mask in _head_masks(NA_WIDTH, NA_HEAD_DIM, NA_HEADS):
        s = _dot_t(jnp.where(mask, q, jnp.zeros_like(q)), k)
        p = jnp.exp2(s - jnp.max(s, axis=-1, keepdims=True))
        o = _dot(p.astype(BF16), v)
        out = jnp.where(mask, o / jnp.sum(p, axis=-1, keepdims=True), out)
    o_ref[0] = out.astype(o_ref.dtype)


def _small_attn(q, k, v):
    B, N, W = q.shape
    spec = pl.BlockSpec((1, N, W), lambda b: (b, 0, 0))
    return pl.pallas_call(
        _small_attn_kernel, grid=(B,), in_specs=[spec, spec, spec], out_specs=spec,
        out_shape=jax.ShapeDtypeStruct((B, N, W), BF16),
        compiler_params=_cparams(("arbitrary",), 32),
        name="ctx_na_attention",
    )(q, k, v)


def _mix_out_kernel(x_ref, gt_ref, ya_ref, yb_ref, u_ref, up_ref, un_ref, wo_ref, wp_ref, ps_ref, o_ref,
                    *, per_seq, seq_len):
    tm = x_ref.shape[0]
    i = pl.program_id(0) % per_seq
    prev = jnp.where(i == 0, 0.0, up_ref[...])
    nxt = jnp.where(i == per_seq - 1, 0.0, un_ref[...])
    ext = jnp.concatenate([prev, u_ref[...], nxt], axis=0)
    lane = lax.broadcasted_iota(jnp.int32, (1, POOL_WIDTH), 1)
    grp = lane // POOL_GROUP_DIM
    half = jnp.where(grp == 0, POOL_WINDOWS[0] // 2,
                     jnp.where(grp == 1, POOL_WINDOWS[1] // 2,
                               jnp.where(grp == 2, POOL_WINDOWS[2] // 2, POOL_WINDOWS[3] // 2)))
    wa = wo_ref[0:MLA_WIDTH, :]
    wb = wo_ref[MLA_WIDTH:MLA_WIDTH + NA_WIDTH, :]
    wc = wo_ref[MLA_WIDTH + NA_WIDTH:, :]
    sub = min(MIX_OUT_SUB_ROWS, tm)
    n = sub + 2 * POOL_HALO
    for s in range(tm // sub):
        rows = slice(s * sub, (s + 1) * sub)
        e = ext[s * sub:s * sub + n]
        p = e + pltpu.roll(e, 1, 0)
        sums = [p]
        for step in (1, 2, 4):
            p = pltpu.roll(p, step, 0) + pltpu.roll(p, n - step, 0)
            sums.append(p)
        sums = [w[POOL_HALO:POOL_HALO + sub] for w in sums]
        win = jnp.where(grp == 0, sums[0], jnp.where(grp == 1, sums[1], jnp.where(grp == 2, sums[2], sums[3])))
        t = i * tm + s * sub + lax.broadcasted_iota(jnp.int32, (sub, 1), 0)
        cnt = jnp.minimum(t + half, seq_len) - jnp.maximum(t - half, 0)
        pooled = win / cnt.astype(F32) - e[POOL_HALO:POOL_HALO + sub]
        yc = _dot(pooled.astype(BF16), wp_ref[...]) * ps_ref[...]
        y = _dot(ya_ref[rows, :], wa) + _dot(yb_ref[rows, :], wb) + _dot(yc.astype(BF16), wc)
        o_ref[rows, :] = x_ref[rows, :] + gt_ref[0] * y


def _mix_out(x, mod, j_gate, ya, yb, u, lw, seq_len):
    T, D = x.shape
    nb = mod[2]
    tm = min(MIX_OUT_ROWS, seq_len)
    per = (T // nb) // tm
    per_seq = seq_len // tm
    hb = tm // POOL_HALO
    last_hb = T // POOL_HALO - 1
    tok = lambda w: pl.BlockSpec((tm, w), lambda i: (i, 0))
    full = _resident
    return pl.pallas_call(
        functools.partial(_mix_out_kernel, per_seq=per_seq, seq_len=seq_len),
        grid=(T // tm,),
        in_specs=[tok(D), _mod_spec(mod, j_gate, per),
                  tok(MLA_WIDTH), tok(NA_WIDTH), tok(POOL_WIDTH),
                  pl.BlockSpec((POOL_HALO, POOL_WIDTH), lambda i: (jnp.maximum(i * hb - 1, 0), 0)),
                  pl.BlockSpec((POOL_HALO, POOL_WIDTH), lambda i: (jnp.minimum((i + 1) * hb, last_hb), 0)),
                  full(lw["w_out"]), full(lw["w_pool"]), full(lw["pool_scale"])],
        out_specs=tok(D),
        out_shape=jax.ShapeDtypeStruct((T, D), F32),
        compiler_params=_cparams(("arbitrary",), 48),
        name="mix_out",
    )(x, mod[0], ya, yb, u, u, u, lw["w_out"], lw["w_pool"], lw["pool_scale"])


def _rope_partner(a):
    lead = a.shape[:-1]
    return a.reshape(lead + (2, 2, MLA_ROPE // 4))[..., ::-1, :].reshape(lead + (MLA_ROPE,))


def _rope_tables(n_tok):
    t = np.arange(n_tok)
    axis_dim = MLA_ROPE // 2
    inv_freq = ROPE_BASE ** (-np.arange(0, axis_dim, 2, dtype=np.float64) / axis_dim)
    ar = (t // GRID_W)[:, None] * inv_freq
    ac = (t % GRID_W)[:, None] * inv_freq
    ones = np.ones((n_tok, MLA_NOPE))
    zeros = np.zeros((n_tok, MLA_NOPE))
    pad = np.zeros((n_tok, MLA_PAD - MLA_QK))
    cos = np.concatenate([ones, np.cos(ar), np.cos(ar), np.cos(ac), np.cos(ac), pad + 1.0], axis=1)
    sin = np.concatenate([zeros, -np.sin(ar), np.sin(ar), -np.sin(ac), np.sin(ac), pad], axis=1)
    return jnp.asarray(cos, F32), jnp.asarray(sin, F32)


def _identity_tables(n_tok):
    return jnp.ones((n_tok, MLA_PAD), F32), jnp.zeros((n_tok, MLA_PAD), F32)


def _layer_weights(l, w_in, g_cq, w_q_up, g_ckv, w_kv_up, g_mla_q, g_mla_k, g_na_q, g_na_k,
                   w_pool, pool_scale, w_out):
    D = w_in.shape[1]
    wi = w_in[l]
    w_kr = wi[:, COL_KR:COL_KR + MLA_ROPE]
    in_rope_lanes = lambda w: jnp.pad(w, ((0, 0), (MLA_NOPE, MLA_PAD - MLA_QK)))
    w_in_ext = jnp.concatenate([wi[:, :COL_KR], in_rope_lanes(w_kr), in_rope_lanes(_rope_partner(w_kr)),
                                wi[:, COL_KR + MLA_ROPE:]], axis=1).astype(BF16)
    wq = w_q_up[l].reshape(MLA_Q_RANK, MLA_HEADS, MLA_QK)
    wq = jnp.concatenate([wq, _rope_partner(wq[:, :, MLA_NOPE:])], axis=2).reshape(MLA_Q_RANK, MLA_HEADS * MLA_PAD)
    wkv = w_kv_up[l].reshape(MLA_KV_RANK, MLA_HEADS, MLA_NOPE + MLA_V)
    wk = jnp.pad(wkv[:, :, :MLA_NOPE], ((0, 0), (0, 0), (0, MLA_PAD - MLA_NOPE)))
    wv = wkv[:, :, MLA_NOPE:]
    w_kv = jnp.concatenate([wk.reshape(MLA_KV_RANK, -1), wv.reshape(MLA_KV_RANK, -1)], axis=1)
    pad_gain = lambda g: jnp.pad(g, (0, MLA_PAD - MLA_QK)).reshape(1, MLA_PAD)
    partner_gain = lambda g: jnp.pad(_rope_partner(g[MLA_NOPE:]), (MLA_NOPE, MLA_PAD - MLA_QK)).reshape(1, MLA_PAD)
    head_id = np.arange(NA_WIDTH) // NA_HEAD_DIM
    bd = jnp.asarray((head_id[:, None] == head_id[None, :]) / NA_HEAD_DIM, BF16)
    wp = jnp.zeros((POOL_WIDTH, POOL_WIDTH), F32)
    for g in range(len(POOL_WINDOWS)):
        sl = slice(g * POOL_GROUP_DIM, (g + 1) * POOL_GROUP_DIM)
        wp = wp.at[sl, sl].set(w_pool[l, g])
    return {
        "w_in": w_in_ext, "g_cq": g_cq[l].reshape(1, -1), "w_q": wq.astype(BF16),
        "g_ckv": g_ckv[l].reshape(1, -1), "w_kv": w_kv.astype(BF16),
        "g_q": pad_gain(g_mla_q[l]), "g_k": pad_gain(g_mla_k[l]),
        "g_q_partner": partner_gain(g_mla_q[l]), "g_k_partner": partner_gain(g_mla_k[l]),
        "bd": bd, "g_nq": jnp.tile(g_na_q[l], NA_HEADS).reshape(1, -1), "g_nk": jnp.tile(g_na_k[l], NA_HEADS).reshape(1, -1),
        "w_pool": wp.astype(BF16), "pool_scale": pool_scale[l].reshape(1, -1), "w_out": w_out[l].astype(BF16),
    }


def kernel(x, c, ctx, c_ctx, w_mod, b_mod, w_ffn_gate, w_ffn_up, w_ffn_down, w_in, g_cq, w_q_up, g_ckv, w_kv_up,
           g_mla_q, g_mla_k, g_na_q, g_na_k, na_rpb, w_pool, pool_scale, w_out):
    B, S, D = x.shape
    C = ctx.shape[1]
    L = w_mod.shape[0]
    assert S % (NA_GROUPS_PER_STEP * NA_GROUP_ROWS * GRID_W) == 0 and S // GRID_W >= NA_KEY_ROWS and B + 1 <= SUBLANES

    cc = jnp.zeros((SUBLANES, D), F32).at[:B].set(c).at[B].set(c_ctx)
    mods = _modulation(cc, w_mod, b_mod).reshape(L * SUBLANES * N_MOD, 1, D)
    rope = _rope_tables(S)
    no_rope = _identity_tables(C)
    ffn_w = (w_ffn_gate.astype(BF16), w_ffn_up.astype(BF16), w_ffn_down.astype(BF16))

    xs = x.reshape(B * S, D)
    xc = ctx.reshape(B * C, D)
    for l in range(L):
        last = l == L - 1
        mx = (mods, l * SUBLANES * N_MOD, B)
        mc = (mods, (l * SUBLANES + B) * N_MOD, 1)
        lw = _layer_weights(l, w_in, g_cq, w_q_up, g_ckv, w_kv_up, g_mla_q, g_mla_k, g_na_q, g_na_k,
                            w_pool, pool_scale, w_out)
        xs = _ffn(xs, mx, (0, 1, 2), ffn_w, l, 0)
        xc = _ffn(xc, mc, (0, 1, 2), ffn_w, l, 0)

        qx, kx, vx, nqx, nkx, nvx, ux = _pre(xs, mx, (3, 4), lw, rope, S)
        qc, kc, vc, nqc, nkc, nvc, uc = _pre(xc, mc, (3, 4), lw, no_rope, C)
        per_b = lambda a, n: a.reshape(B, n, a.shape[-1])
        bound = _mla_score_bound(lw["g_q"], lw["g_k"])[0, 0]
        kv_c = (per_b(kc, C), per_b(vc, C))
        ya = _mla_attention(per_b(qx, S), [kv_c, (per_b(kx, S), per_b(vx, S))], bound)
        yb = _na(per_b(nqx, S), per_b(nkx, S), per_b(nvx, S), per_b(nkc, C), per_b(nvc, C),
                 na_rpb[l], g_na_q[l], g_na_k[l])
        xs = _mix_out(xs, mx, 5, ya.reshape(B * S, -1), yb.reshape(B * S, -1), ux, lw, S)
        xs = _ffn(xs, mx, (6, 7, 8), ffn_w, l, 1)
        if not last:
            ya_c = _mla_attention(per_b(qc, C), [kv_c], bound)
            yb_c = _small_attn(per_b(nqc, C), per_b(nkc, C), per_b(nvc, C))
            xc = _mix_out(xc, mc, 5, ya_c.reshape(B * C, -1), yb_c.reshape(B * C, -1), uc, lw, C)
            xc = _ffn(xc, mc, (6, 7, 8), ffn_w, l, 1)
    return xs.reshape(B, S, D)
```

```python
import functools
import math

import numpy as np
import jax
import jax.numpy as jnp
from jax import lax
from jax.experimental import pallas as pl
from jax.experimental.pallas import tpu as pltpu

F32 = jnp.float32
BF16 = jnp.bfloat16

LANES = 128
SUBLANES = 8

GRID_W = 64
MLA_HEADS = 8
MLA_NOPE = 64
MLA_ROPE = 32
MLA_QK = MLA_NOPE + MLA_ROPE
MLA_V = 64
MLA_Q_RANK = 768
MLA_KV_RANK = 256
MLA_WIDTH = MLA_HEADS * MLA_V
MLA_SCALE = 1.0 / math.sqrt(MLA_QK)
MLA_PAD = LANES
LOG2E = math.log2(math.e)
MLA_SCALE_LOG2 = MLA_SCALE * LOG2E
MLA_SHIFT_LANE = MLA_QK
MLA_FAST_BOUND = 60.0
MLA_KEY_TILE = 256
NA_HEADS = 4
NA_HEAD_DIM = 64
NA_WIDTH = NA_HEADS * NA_HEAD_DIM
NA_WIN_R = 8
NA_WIN_C = 16
NA_SCALE = 1.0 / math.sqrt(NA_HEAD_DIM)
NA_FAST_SPAN = 120.0
NA_GROUPS_PER_STEP = 4
NA_GROUP_ROWS = 4
NA_KEY_ROWS = NA_GROUP_ROWS + NA_WIN_R
POOL_WINDOWS = (2, 4, 8, 16)
POOL_GROUP_DIM = 64
POOL_WIDTH = len(POOL_WINDOWS) * POOL_GROUP_DIM
POOL_HALO = max(POOL_WINDOWS) // 2
FFN_CHUNK = 256
MIX_IN_ROWS = 512
MIX_IN_SUB_ROWS = 256
MIX_OUT_ROWS = 1024
MIX_OUT_SUB_ROWS = 256
N_MOD = 9
ROPE_BASE = 10000.0
EPS = 1e-6
NEG_BIG = -1e30

COL_CQ = 0
COL_CKV = COL_CQ + MLA_Q_RANK
COL_KR = COL_CKV + MLA_KV_RANK
COL_KRS = COL_KR + MLA_PAD
COL_NQ = COL_KRS + MLA_PAD
COL_NK = COL_NQ + NA_WIDTH
COL_NV = COL_NK + NA_WIDTH
COL_U = COL_NV + NA_WIDTH
IN_EXT = COL_U + POOL_WIDTH


def _dot(a, b):
    return jnp.dot(a, b, preferred_element_type=F32)


def _dot_t(a, b):
    return lax.dot_general(a, b, (((1,), (1,)), ((), ())), preferred_element_type=F32)


def _silu(x):
    return x * jax.nn.sigmoid(x)


def _cparams(sem, vmem_mb):
    return pltpu.CompilerParams(dimension_semantics=sem, vmem_limit_bytes=vmem_mb * 1024 * 1024)


def _mod_kernel(c_ref, w_ref, b_ref, o_ref):
    h = _silu(c_ref[...]).astype(BF16)
    o_ref[0] = _dot(h, w_ref[0].astype(BF16)) + b_ref[0]


def _modulation(cc, w_mod, b_mod):
    L, D, NM = w_mod.shape
    tn = 1024
    return pl.pallas_call(
        _mod_kernel,
        grid=(L, NM // tn),
        in_specs=[pl.BlockSpec((SUBLANES, D), lambda l, j: (0, 0)),
                  pl.BlockSpec((1, D, tn), lambda l, j: (l, 0, j)),
                  pl.BlockSpec((1, 1, tn), lambda l, j: (l, 0, j))],
        out_specs=pl.BlockSpec((1, SUBLANES, tn), lambda l, j: (l, 0, j)),
        out_shape=jax.ShapeDtypeStruct((L, SUBLANES, NM), F32),
        compiler_params=_cparams(("arbitrary", "arbitrary"), 32),
        name="modulation",
    )(cc, w_mod, b_mod.reshape(L, 1, NM))


def _modulated_norm(x, shift, scale):
    ms = jnp.mean(x * x, axis=-1, keepdims=True)
    return (x * lax.rsqrt(ms + EPS)) * (1.0 + scale) + shift


def _ffn_kernel(x_ref, sh_ref, sc_ref, gt_ref, wg_ref, wu_ref, wd_ref, o_ref):
    x = x_ref[...]
    h = _modulated_norm(x, sh_ref[0], sc_ref[0]).astype(BF16)
    acc = None
    for c in range(wg_ref.shape[1] // FFN_CHUNK):
        cols = slice(c * FFN_CHUNK, (c + 1) * FFN_CHUNK)
        g = _dot(h, wg_ref[:, cols])
        u = _dot(h, wu_ref[:, cols])
        y = _dot((_silu(g) * u).astype(BF16), wd_ref[cols, :])
        acc = y if acc is None else acc + y
    o_ref[...] = x + (0.5 * gt_ref[0]) * acc


def _resident(a):
    return pl.BlockSpec(a.shape, lambda i: (0,) * a.ndim, pipeline_mode=pl.Buffered(1))


def _mod_spec(mod, j, per):
    table, first, _ = mod
    return pl.BlockSpec((1, 1, table.shape[-1]), lambda i: (first + (i // per) * N_MOD + j, 0, 0))


def _ffn(x, mod, js, weights, layer, half):
    T, D = x.shape
    nb = mod[2]
    tm = min(512, T // nb)
    per = (T // nb) // tm
    whole = lambda a: pl.BlockSpec((None, None) + a.shape[2:], lambda i: (layer, half, 0, 0),
                                   pipeline_mode=pl.Buffered(1))
    return pl.pallas_call(
        _ffn_kernel,
        grid=(T // tm,),
        in_specs=[pl.BlockSpec((tm, D), lambda i: (i, 0))] + [_mod_spec(mod, j, per) for j in js]
                 + [whole(a) for a in weights],
        out_specs=pl.BlockSpec((tm, D), lambda i: (i, 0)),
        out_shape=jax.ShapeDtypeStruct((T, D), F32),
        compiler_params=_cparams(("arbitrary",), 48),
        name="ffn",
    )(x, mod[0], mod[0], mod[0], *weights)


def _gain_norm(x, gain, width):
    ms = jnp.sum(x * x, axis=-1, keepdims=True) * (1.0 / width)
    return x * lax.rsqrt(ms + EPS) * gain


def _head_rsqrt(sumsq):
    return lax.rsqrt(sumsq * (1.0 / MLA_QK) + EPS)


def _mla_score_bound(gq, gk):
    amax = lambda g: jnp.max(jnp.abs(g), axis=-1, keepdims=True)
    return (MLA_SCALE_LOG2 * MLA_QK) * amax(gq) * amax(gk)


def _group_mean_square(x, bd):
    xx = x * x
    hi = xx.astype(BF16)
    lo = (xx - hi.astype(F32)).astype(BF16)
    return _dot(hi, bd) + _dot(lo, bd)


def _pre_kernel(x_ref, sh_ref, sc_ref, win_ref, gcq_ref, wq_ref, gckv_ref, wkv_ref, gq_ref, gqs_ref, gk_ref, gks_ref,
                cos_ref, sin_ref, bd_ref, gnq_ref, gnk_ref,
                q_ref, k_ref, v_ref, nq_ref, nk_ref, nv_ref, u_ref, qf_scr, kvf_scr, zt_scr, *, sub_rows):
    lane = lax.broadcasted_iota(jnp.int32, (1, MLA_PAD), 1)
    real = lane < MLA_QK
    shift_lane = lane == MLA_SHIFT_LANE
    gq, gk = gq_ref[...], gk_ref[...]
    neg_bound = -_mla_score_bound(gq, gk)
    ones = jnp.ones((sub_rows, MLA_PAD), BF16)
    n_sub = x_ref.shape[0] // sub_rows

    @pl.when(pl.program_id(0) == 0)
    def _():
        qf_scr[...] = jnp.zeros_like(qf_scr)
        kvf_scr[...] = jnp.zeros_like(kvf_scr)
        zt_scr[...] = jnp.zeros_like(zt_scr)

    zt = lambda rows, lo, hi: zt_scr[rows, lo - COL_KR:hi - COL_KR]
    for s in range(n_sub):
        rows = slice(s * sub_rows, (s + 1) * sub_rows)
        cos, sin = cos_ref[rows, :], sin_ref[rows, :]
        qf = qf_scr[rows, :]
        q_cos = cos * (gq * MLA_SCALE_LOG2)
        q_sin = sin * (gqs_ref[...] * MLA_SCALE_LOG2)
        for hd in range(MLA_HEADS):
            sl = slice(hd * MLA_PAD, (hd + 1) * MLA_PAD)
            x = qf[:, sl]
            r = _head_rsqrt(jnp.sum(jnp.where(real, x * x, 0.0), axis=-1, keepdims=True))
            qh = (x * q_cos + pltpu.roll(x, MLA_PAD - MLA_ROPE, 1) * q_sin) * r
            q_ref[rows, sl] = jnp.where(shift_lane, neg_bound, qh).astype(BF16)

        kvf = kvf_scr[rows, :]
        kr = zt(rows, COL_KR, COL_KRS)
        k_rot = kr * (cos * gk) + zt(rows, COL_KRS, COL_NQ) * (sin * gks_ref[...])
        kr_sq = jnp.sum(kr * kr, axis=-1, keepdims=True)
        for hd in range(MLA_HEADS):
            sl = slice(hd * MLA_PAD, (hd + 1) * MLA_PAD)
            x = kvf[:, sl]
            r = _head_rsqrt(jnp.sum(x * x, axis=-1, keepdims=True) + kr_sq)
            k_ref[rows, sl] = jnp.where(shift_lane, 1.0, (x * gk + k_rot) * r).astype(BF16)
        for pr in range(MLA_HEADS // 2):
            vcols = slice(MLA_HEADS * MLA_PAD + pr * MLA_PAD, MLA_HEADS * MLA_PAD + (pr + 1) * MLA_PAD)
            v_ref[rows, 2 * pr * MLA_PAD:(2 * pr + 1) * MLA_PAD] = kvf[:, vcols].astype(BF16)
            v_ref[rows, (2 * pr + 1) * MLA_PAD:(2 * pr + 2) * MLA_PAD] = ones

        bd = bd_ref[...]
        nq = zt(rows, COL_NQ, COL_NK)
        nq_ref[rows, :] = (nq * lax.rsqrt(_group_mean_square(nq, bd) + EPS) * (gnq_ref[...] * (NA_SCALE * LOG2E))).astype(BF16)
        nk = zt(rows, COL_NK, COL_NV)
        nk_ref[rows, :] = (nk * lax.rsqrt(_group_mean_square(nk, bd) + EPS) * gnk_ref[...]).astype(BF16)
        nv_ref[rows, :] = zt(rows, COL_NV, COL_U).astype(BF16)
        u_ref[rows, :] = zt(rows, COL_U, IN_EXT)

    for s in range(n_sub):
        rows = slice(s * sub_rows, (s + 1) * sub_rows)
        h = _modulated_norm(x_ref[rows, :], sh_ref[0], sc_ref[0]).astype(BF16)
        z = _dot(h, win_ref[...])
        cq = _gain_norm(z[:, COL_CQ:COL_CKV], gcq_ref[...], MLA_Q_RANK).astype(BF16)
        qf_scr[rows, :] = _dot(cq, wq_ref[...])
        ckv = _gain_norm(z[:, COL_CKV:COL_KR], gckv_ref[...], MLA_KV_RANK).astype(BF16)
        kvf_scr[rows, :] = _dot(ckv, wkv_ref[...])
        zt_scr[rows, :] = z[:, COL_KR:]


def _pre(x, mod, js, lw, tables, seq_len):
    T, D = x.shape
    nb = mod[2]
    tm = min(MIX_IN_ROWS, seq_len)
    n = T // tm
    per = (T // nb) // tm
    per_seq = seq_len // tm
    cur = lambda i: jnp.minimum(i, n - 1)
    prv = lambda i: jnp.maximum(i - 1, 0)
    first = mod[1]
    mod_spec = lambda j: pl.BlockSpec((1, 1, D), lambda i: (first + (cur(i) // per) * N_MOD + j, 0, 0))
    out = lambda w: pl.BlockSpec((tm, w), lambda i: (prv(i), 0))
    full = _resident
    pos = pl.BlockSpec((tm, LANES), lambda i: (prv(i) % per_seq, 0))
    weights = (lw["w_in"], lw["g_cq"], lw["w_q"], lw["g_ckv"], lw["w_kv"],
               lw["g_q"], lw["g_q_partner"], lw["g_k"], lw["g_k_partner"])
    tail = (lw["bd"], lw["g_nq"], lw["g_nk"])
    hq = MLA_HEADS * MLA_PAD
    return pl.pallas_call(
        functools.partial(_pre_kernel, sub_rows=min(MIX_IN_SUB_ROWS, tm)),
        grid=(n + 1,),
        in_specs=[pl.BlockSpec((tm, D), lambda i: (cur(i), 0))] + [mod_spec(j) for j in js]
                 + [full(a) for a in weights] + [pos, pos] + [full(a) for a in tail],
        out_specs=[out(hq), out(hq), out(hq), out(NA_WIDTH), out(NA_WIDTH), out(NA_WIDTH), out(POOL_WIDTH)],
        out_shape=[jax.ShapeDtypeStruct((T, hq), BF16), jax.ShapeDtypeStruct((T, hq), BF16),
                   jax.ShapeDtypeStruct((T, hq), BF16), jax.ShapeDtypeStruct((T, NA_WIDTH), BF16),
                   jax.ShapeDtypeStruct((T, NA_WIDTH), BF16), jax.ShapeDtypeStruct((T, NA_WIDTH), BF16),
                   jax.ShapeDtypeStruct((T, POOL_WIDTH), F32)],
        scratch_shapes=[pltpu.VMEM((tm, hq), F32), pltpu.VMEM((tm, hq + MLA_WIDTH), F32),
                        pltpu.VMEM((tm, IN_EXT - COL_KR), F32)],
        compiler_params=_cparams(("arbitrary",), 56),
        name="mix_in",
    )(x, mod[0], mod[0], *weights, *tables, *tail)


def _flash_fast_kernel(q_ref, *refs):
    o_ref = refs[-1]
    tq = q_ref.shape[1]
    lanes = [slice(hd * MLA_PAD, (hd + 1) * MLA_PAD) for hd in range(2)]
    q = [q_ref[0, :, sl] for sl in lanes]
    acc = [jnp.zeros((tq, 2 * MLA_PAD), F32) for _ in range(2)]
    for k_ref, v_ref in zip(refs[0:-1:2], refs[1:-1:2]):
        for t in range(k_ref.shape[1] // MLA_KEY_TILE):
            keys = slice(t * MLA_KEY_TILE, (t + 1) * MLA_KEY_TILE)
            for hd in range(2):
                p = jnp.exp2(_dot_t(q[hd], k_ref[0, keys, lanes[hd]])).astype(BF16)
                acc[hd] = acc[hd] + _dot(p, v_ref[0, keys, :])
    for hd in range(2):
        own = slice(hd * MLA_V, (hd + 1) * MLA_V)
        o_ref[0, :, own] = (acc[hd][:, own] / acc[hd][:, MLA_PAD:MLA_PAD + 1]).astype(o_ref.dtype)


def _flash_fast(q, kv):
    B, Nq, HQ = q.shape
    pairs = HQ // (2 * MLA_PAD)
    tq = min(1024, Nq)
    kv_specs = [pl.BlockSpec((1, a.shape[1], 2 * MLA_PAD), lambda b, h, i: (b, 0, h)) for pair in kv for a in pair]
    return pl.pallas_call(
        _flash_fast_kernel,
        grid=(B, pairs, Nq // tq),
        in_specs=[pl.BlockSpec((1, tq, 2 * MLA_PAD), lambda b, h, i: (b, i, h))] + kv_specs,
        out_specs=pl.BlockSpec((1, tq, 2 * MLA_V), lambda b, h, i: (b, i, h)),
        out_shape=jax.ShapeDtypeStruct((B, Nq, pairs * 2 * MLA_V), BF16),
        compiler_params=_cparams(("arbitrary", "arbitrary", "arbitrary"), 56),
        name="mla_attention",
    )(q, *[a for pair in kv for a in pair])


def _flash_safe_kernel(q_ref, k_ref, v_ref, o_ref, *, tk):
    tq = q_ref.shape[1]
    nk = k_ref.shape[1] // tk
    for hd in range(2):
        lanes = slice(hd * MLA_PAD, (hd + 1) * MLA_PAD)
        q = q_ref[0, :, lanes]

        def body(j, carry):
            m, acc = carry
            start = pl.multiple_of(j * tk, tk)
            s = _dot_t(q, k_ref[0, pl.ds(start, tk), lanes])
            m_new = jnp.maximum(m, jnp.max(s, axis=-1, keepdims=True))
            p = jnp.exp2(s - m_new)
            acc = jnp.exp2(m - m_new) * acc + _dot(p.astype(BF16), v_ref[0, pl.ds(start, tk), :])
            return m_new, acc

        init = (jnp.full((tq, 1), NEG_BIG, F32), jnp.zeros((tq, 2 * MLA_PAD), F32))
        _, acc = lax.fori_loop(0, nk, body, init)
        own = slice(hd * MLA_V, (hd + 1) * MLA_V)
        o_ref[0, :, own] = (acc[:, own] / acc[:, MLA_PAD:MLA_PAD + 1]).astype(o_ref.dtype)


def _flash_safe(q, kv):
    k = jnp.concatenate([pair[0] for pair in kv], axis=1)
    v = jnp.concatenate([pair[1] for pair in kv], axis=1)
    B, Nq, HQ = q.shape
    Nk = k.shape[1]
    pairs = HQ // (2 * MLA_PAD)
    tq = min(512, Nq)
    tk = next(t for t in (768, 512, 256) if Nk % t == 0)
    kv_spec = pl.BlockSpec((1, Nk, 2 * MLA_PAD), lambda b, h, i: (b, 0, h))
    return pl.pallas_call(
        functools.partial(_flash_safe_kernel, tk=tk),
        grid=(B, pairs, Nq // tq),
        in_specs=[pl.BlockSpec((1, tq, 2 * MLA_PAD), lambda b, h, i: (b, i, h)), kv_spec, kv_spec],
        out_specs=pl.BlockSpec((1, tq, 2 * MLA_V), lambda b, h, i: (b, i, h)),
        out_shape=jax.ShapeDtypeStruct((B, Nq, pairs * 2 * MLA_V), BF16),
        compiler_params=_cparams(("arbitrary", "arbitrary", "arbitrary"), 56),
        name="mla_attention_running_max",
    )(q, k, v)


def _mla_attention(q, kv, bound):
    return lax.cond(bound <= MLA_FAST_BOUND, lambda: _flash_fast(q, kv), lambda: _flash_safe(q, kv))


def _head_masks(width, head_dim, heads):
    lane = lax.broadcasted_iota(jnp.int32, (1, width), 1)
    return [(lane >= h * head_dim) & (lane < (h + 1) * head_dim) for h in range(heads)]


def _na_kernel(q_ref, k_ref, v_ref, kc_ref, vc_ref, bias_ref, bound_ref, o_ref, *, running_max):
    rows = k_ref.shape[1] // GRID_W
    groups = rows // NA_GROUP_ROWS
    tq = NA_GROUP_ROWS * GRID_W
    kc, vc = kc_ref[0], vc_ref[0]
    bound = bound_ref[...]
    masks = _head_masks(NA_WIDTH, NA_HEAD_DIM, NA_HEADS)
    for gi in range(q_ref.shape[1] // tq):
        g = pl.program_id(1) * (q_ref.shape[1] // tq) + gi
        variant = jnp.where(g == 0, 0, jnp.where(g == groups - 1, 2, 1))
        base = jnp.clip(g * NA_GROUP_ROWS - NA_WIN_R // 2, 0, rows - NA_KEY_ROWS)
        start = pl.multiple_of(base * GRID_W, GRID_W)
        kw = k_ref[0, pl.ds(start, NA_KEY_ROWS * GRID_W), :]
        vw = v_ref[0, pl.ds(start, NA_KEY_ROWS * GRID_W), :]
        q = q_ref[0, gi * tq:(gi + 1) * tq, :]
        out = jnp.zeros(q.shape, F32)
        for hd, mask in enumerate(masks):
            qh = jnp.where(mask, q, jnp.zeros_like(q))
            s_loc = _dot_t(qh, kw) + bias_ref[variant, hd]
            s_ctx = _dot_t(qh, kc) - bound
            if running_max:
                m = jnp.maximum(jnp.max(s_loc, axis=-1, keepdims=True), jnp.max(s_ctx, axis=-1, keepdims=True))
                s_loc, s_ctx = s_loc - m, s_ctx - m
            p_loc = jnp.exp2(s_loc)
            p_ctx = jnp.exp2(s_ctx)
            l = jnp.sum(p_loc, axis=-1, keepdims=True) + jnp.sum(p_ctx, axis=-1, keepdims=True)
            o = _dot(p_loc.astype(BF16), vw) + _dot(p_ctx.astype(BF16), vc)
            out = jnp.where(mask, o / l, out)
        o_ref[0, gi * tq:(gi + 1) * tq, :] = out.astype(o_ref.dtype)


def _na_call(q, k, v, kc, vc, bias, bound, running_max):
    B, S, W = q.shape
    C = kc.shape[1]
    tq = NA_GROUP_ROWS * GRID_W * NA_GROUPS_PER_STEP
    return pl.pallas_call(
        functools.partial(_na_kernel, running_max=running_max),
        grid=(B, S // tq),
        in_specs=[pl.BlockSpec((1, tq, W), lambda b, g: (b, g, 0)),
                  pl.BlockSpec((1, S, W), lambda b, g: (b, 0, 0)),
                  pl.BlockSpec((1, S, W), lambda b, g: (b, 0, 0)),
                  pl.BlockSpec((1, C, W), lambda b, g: (b, 0, 0)),
                  pl.BlockSpec((1, C, W), lambda b, g: (b, 0, 0)),
                  pl.BlockSpec(bias.shape, lambda b, g: (0, 0, 0, 0), pipeline_mode=pl.Buffered(1)),
                  pl.BlockSpec((1, 1), lambda b, g: (0, 0))],
        out_specs=pl.BlockSpec((1, tq, W), lambda b, g: (b, g, 0)),
        out_shape=jax.ShapeDtypeStruct((B, S, W), BF16),
        compiler_params=_cparams(("arbitrary", "arbitrary"), 56),
        name="na_attention_running_max" if running_max else "na_attention",
    )(q, k, v, kc, vc, bias, bound.reshape(1, 1))


def _na(q, k, v, kc, vc, rpb, g_nq, g_nk):
    qk = (LOG2E * NA_SCALE * NA_HEAD_DIM) * jnp.max(jnp.abs(g_nq)) * jnp.max(jnp.abs(g_nk))
    hi = LOG2E * jnp.maximum(jnp.max(rpb), 0.0)
    lo = LOG2E * jnp.minimum(jnp.min(rpb), 0.0)
    bound = qk + hi
    bias = _na_bias_tables(rpb, bound)
    args = (q, k, v, kc, vc, bias, bound)
    return lax.cond(2.0 * qk + hi - lo <= NA_FAST_SPAN,
                    lambda: _na_call(*args, running_max=False), lambda: _na_call(*args, running_max=True))


def _na_bias_tables(rpb, bound):
    n_c = 2 * NA_WIN_C - 1
    w = np.arange(GRID_W)[:, None]
    kcol = np.arange(GRID_W)[None, :]
    cs = np.clip(w - NA_WIN_C // 2, 0, GRID_W - NA_WIN_C)
    col_ok = (kcol >= cs) & (kcol < cs + NA_WIN_C)
    col_sel = (col_ok[..., None] & ((kcol - w + NA_WIN_C - 1)[..., None] == np.arange(n_c))).astype(np.float32)
    blocks = jnp.einsum("hrc,wkc->hrwk", rpb, col_sel, precision=lax.Precision.HIGHEST)
    blocks = jnp.where(col_ok, LOG2E * blocks - bound, NEG_BIG)
    masked = jnp.full((rpb.shape[0], GRID_W, GRID_W), NEG_BIG, F32)
    half = NA_WIN_R // 2
    variants = []
    for v in range(3):
        group = []
        for i in range(NA_GROUP_ROWS):
            q_rel, r0 = ((i, 0), (half + i, i), (NA_WIN_R + i, half))[v]
            row = [blocks[:, j - q_rel + NA_WIN_R - 1] if r0 <= j < r0 + NA_WIN_R else masked
                   for j in range(NA_KEY_ROWS)]
            group.append(jnp.concatenate(row, axis=-1))
        variants.append(jnp.concatenate(group, axis=1))
    return jnp.stack(variants).astype(F32)


def _small_attn_kernel(q_ref, k_ref, v_ref, o_ref):
    q, k, v = q_ref[0], k_ref[0], v_ref[0]
    out = jnp.zeros(q.shape, F32)
    for mask in _head_masks(NA_WIDTH, NA_HEAD_DIM, NA_HEADS):
        s = _dot_t(jnp.where(mask, q, jnp.zeros_like(q)), k)
        p = jnp.exp2(s - jnp.max(s, axis=-1, keepdims=True))
        o = _dot(p.astype(BF16), v)
        out = jnp.where(mask, o / jnp.sum(p, axis=-1, keepdims=True), out)
    o_ref[0] = out.astype(o_ref.dtype)


def _small_attn(q, k, v):
    B, N, W = q.shape
    spec = pl.BlockSpec((1, N, W), lambda b: (b, 0, 0))
    return pl.pallas_call(
        _small_attn_kernel, grid=(B,), in_specs=[spec, spec, spec], out_specs=spec,
        out_shape=jax.ShapeDtypeStruct((B, N, W), BF16),
        compiler_params=_cparams(("arbitrary",), 32),
        name="ctx_na_attention",
    )(q, k, v)


def _mix_out_kernel(x_ref, gt_ref, ya_ref, yb_ref, u_ref, up_ref, un_ref, wo_ref, wp_ref, ps_ref, o_ref,
                    *, per_seq, seq_len):
    tm = x_ref.shape[0]
    i = pl.program_id(0) % per_seq
    prev = jnp.where(i == 0, 0.0, up_ref[...])
    nxt = jnp.where(i == per_seq - 1, 0.0, un_ref[...])
    ext = jnp.concatenate([prev, u_ref[...], nxt], axis=0)
    lane = lax.broadcasted_iota(jnp.int32, (1, POOL_WIDTH), 1)
    grp = lane // POOL_GROUP_DIM
    half = jnp.where(grp == 0, POOL_WINDOWS[0] // 2,
                     jnp.where(grp == 1, POOL_WINDOWS[1] // 2,
                               jnp.where(grp == 2, POOL_WINDOWS[2] // 2, POOL_WINDOWS[3] // 2)))
    wa = wo_ref[0:MLA_WIDTH, :]
    wb = wo_ref[MLA_WIDTH:MLA_WIDTH + NA_WIDTH, :]
    wc = wo_ref[MLA_WIDTH + NA_WIDTH:, :]
    sub = min(MIX_OUT_SUB_ROWS, tm)
    n = sub + 2 * POOL_HALO
    for s in range(tm // sub):
        rows = slice(s * sub, (s + 1) * sub)
        e = ext[s * sub:s * sub + n]
        p = e + pltpu.roll(e, 1, 0)
        sums = [p]
        for step in (1, 2, 4):
            p = pltpu.roll(p, step, 0) + pltpu.roll(p, n - step, 0)
            sums.append(p)
        sums = [w[POOL_HALO:POOL_HALO + sub] for w in sums]
        win = jnp.where(grp == 0, sums[0], jnp.where(grp == 1, sums[1], jnp.where(grp == 2, sums[2], sums[3])))
        t = i * tm + s * sub + lax.broadcasted_iota(jnp.int32, (sub, 1), 0)
        cnt = jnp.minimum(t + half, seq_len) - jnp.maximum(t - half, 0)
        pooled = win / cnt.astype(F32) - e[POOL_HALO:POOL_HALO + sub]
        yc = _dot(pooled.astype(BF16), wp_ref[...]) * ps_ref[...]
        y = _dot(ya_ref[rows, :], wa) + _dot(yb_ref[rows, :], wb) + _dot(yc.astype(BF16), wc)
        o_ref[rows, :] = x_ref[rows, :] + gt_ref[0] * y


def _mix_out(x, mod, j_gate, ya, yb, u, lw, seq_len):
    T, D = x.shape
    nb = mod[2]
    tm = min(MIX_OUT_ROWS, seq_len)
    per = (T // nb) // tm
    per_seq = seq_len // tm
    hb = tm // POOL_HALO
    last_hb = T // POOL_HALO - 1
    tok = lambda w: pl.BlockSpec((tm, w), lambda i: (i, 0))
    full = _resident
    return pl.pallas_call(
        functools.partial(_mix_out_kernel, per_seq=per_seq, seq_len=seq_len),
        grid=(T // tm,),
        in_specs=[tok(D), _mod_spec(mod, j_gate, per),
                  tok(MLA_WIDTH), tok(NA_WIDTH), tok(POOL_WIDTH),
                  pl.BlockSpec((POOL_HALO, POOL_WIDTH), lambda i: (jnp.maximum(i * hb - 1, 0), 0)),
                  pl.BlockSpec((POOL_HALO, POOL_WIDTH), lambda i: (jnp.minimum((i + 1) * hb, last_hb), 0)),
                  full(lw["w_out"]), full(lw["w_pool"]), full(lw["pool_scale"])],
        out_specs=tok(D),
        out_shape=jax.ShapeDtypeStruct((T, D), F32),
        compiler_params=_cparams(("arbitrary",), 48),
        name="mix_out",
    )(x, mod[0], ya, yb, u, u, u, lw["w_out"], lw["w_pool"], lw["pool_scale"])


def _rope_partner(a):
    lead = a.shape[:-1]
    return a.reshape(lead + (2, 2, MLA_ROPE // 4))[..., ::-1, :].reshape(lead + (MLA_ROPE,))


def _rope_tables(n_tok):
    t = np.arange(n_tok)
    axis_dim = MLA_ROPE // 2
    inv_freq = ROPE_BASE ** (-np.arange(0, axis_dim, 2, dtype=np.float64) / axis_dim)
    ar = (t // GRID_W)[:, None] * inv_freq
    ac = (t % GRID_W)[:, None] * inv_freq
    ones = np.ones((n_tok, MLA_NOPE))
    zeros = np.zeros((n_tok, MLA_NOPE))
    pad = np.zeros((n_tok, MLA_PAD - MLA_QK))
    cos = np.concatenate([ones, np.cos(ar), np.cos(ar), np.cos(ac), np.cos(ac), pad + 1.0], axis=1)
    sin = np.concatenate([zeros, -np.sin(ar), np.sin(ar), -np.sin(ac), np.sin(ac), pad], axis=1)
    return jnp.asarray(cos, F32), jnp.asarray(sin, F32)


def _identity_tables(n_tok):
    return jnp.ones((n_tok, MLA_PAD), F32), jnp.zeros((n_tok, MLA_PAD), F32)


def _layer_weights(l, w_in, g_cq, w_q_up, g_ckv, w_kv_up, g_mla_q, g_mla_k, g_na_q, g_na_k,
                   w_pool, pool_scale, w_out):
    D = w_in.shape[1]
    wi = w_in[l]
    w_kr = wi[:, COL_KR:COL_KR + MLA_ROPE]
    in_rope_lanes = lambda w: jnp.pad(w, ((0, 0), (MLA_NOPE, MLA_PAD - MLA_QK)))
    w_in_ext = jnp.concatenate([wi[:, :COL_KR], in_rope_lanes(w_kr), in_rope_lanes(_rope_partner(w_kr)),
                                wi[:, COL_KR + MLA_ROPE:]], axis=1).astype(BF16)
    wq = w_q_up[l].reshape(MLA_Q_RANK, MLA_HEADS, MLA_QK)
    wq = jnp.concatenate([wq, _rope_partner(wq[:, :, MLA_NOPE:])], axis=2).reshape(MLA_Q_RANK, MLA_HEADS * MLA_PAD)
    wkv = w_kv_up[l].reshape(MLA_KV_RANK, MLA_HEADS, MLA_NOPE + MLA_V)
    wk = jnp.pad(wkv[:, :, :MLA_NOPE], ((0, 0), (0, 0), (0, MLA_PAD - MLA_NOPE)))
    wv = wkv[:, :, MLA_NOPE:]
    w_kv = jnp.concatenate([wk.reshape(MLA_KV_RANK, -1), wv.reshape(MLA_KV_RANK, -1)], axis=1)
    pad_gain = lambda g: jnp.pad(g, (0, MLA_PAD - MLA_QK)).reshape(1, MLA_PAD)
    partner_gain = lambda g: jnp.pad(_rope_partner(g[MLA_NOPE:]), (MLA_NOPE, MLA_PAD - MLA_QK)).reshape(1, MLA_PAD)
    head_id = np.arange(NA_WIDTH) // NA_HEAD_DIM
    bd = jnp.asarray((head_id[:, None] == head_id[None, :]) / NA_HEAD_DIM, BF16)
    wp = jnp.zeros((POOL_WIDTH, POOL_WIDTH), F32)
    for g in range(len(POOL_WINDOWS)):
        sl = slice(g * POOL_GROUP_DIM, (g + 1) * POOL_GROUP_DIM)
        wp = wp.at[sl, sl].set(w_pool[l, g])
    return {
        "w_in": w_in_ext, "g_cq": g_cq[l].reshape(1, -1), "w_q": wq.astype(BF16),
        "g_ckv": g_ckv[l].reshape(1, -1), "w_kv": w_kv.astype(BF16),
        "g_q": pad_gain(g_mla_q[l]), "g_k": pad_gain(g_mla_k[l]),
        "g_q_partner": partner_gain(g_mla_q[l]), "g_k_partner": partner_gain(g_mla_k[l]),
        "bd": bd, "g_nq": jnp.tile(g_na_q[l], NA_HEADS).reshape(1, -1), "g_nk": jnp.tile(g_na_k[l], NA_HEADS).reshape(1, -1),
        "w_pool": wp.astype(BF16), "pool_scale": pool_scale[l].reshape(1, -1), "w_out": w_out[l].astype(BF16),
    }


def kernel(x, c, ctx, c_ctx, w_mod, b_mod, w_ffn_gate, w_ffn_up, w_ffn_down, w_in, g_cq, w_q_up, g_ckv, w_kv_up,
           g_mla_q, g_mla_k, g_na_q, g_na_k, na_rpb, w_pool, pool_scale, w_out):
    B, S, D = x.shape
    C = ctx.shape[1]
    L = w_mod.shape[0]
    assert S % (NA_GROUPS_PER_STEP * NA_GROUP_ROWS * GRID_W) == 0 and S // GRID_W >= NA_KEY_ROWS and B + 1 <= SUBLANES

    cc = jnp.zeros((SUBLANES, D), F32).at[:B].set(c).at[B].set(c_ctx)
    mods = _modulation(cc, w_mod, b_mod).reshape(L * SUBLANES * N_MOD, 1, D)
    rope = _rope_tables(S)
    no_rope = _identity_tables(C)
    ffn_w = (w_ffn_gate.astype(BF16), w_ffn_up.astype(BF16), w_ffn_down.astype(BF16))

    xs = x.reshape(B * S, D)
    xc = ctx.reshape(B * C, D)
    for l in range(L):
        last = l == L - 1
        mx = (mods, l * SUBLANES * N_MOD, B)
        mc = (mods, (l * SUBLANES + B) * N_MOD, 1)
        lw = _layer_weights(l, w_in, g_cq, w_q_up, g_ckv, w_kv_up, g_mla_q, g_mla_k, g_na_q, g_na_k,
                            w_pool, pool_scale, w_out)
        xs = _ffn(xs, mx, (0, 1, 2), ffn_w, l, 0)
        xc = _ffn(xc, mc, (0, 1, 2), ffn_w, l, 0)

        qx, kx, vx, nqx, nkx, nvx, ux = _pre(xs, mx, (3, 4), lw, rope, S)
        qc, kc, vc, nqc, nkc, nvc, uc = _pre(xc, mc, (3, 4), lw, no_rope, C)
        per_b = lambda a, n: a.reshape(B, n, a.shape[-1])
        bound = _mla_score_bound(lw["g_q"], lw["g_k"])[0, 0]
        kv_c = (per_b(kc, C), per_b(vc, C))
        ya = _mla_attention(per_b(qx, S), [kv_c, (per_b(kx, S), per_b(vx, S))], bound)
        yb = _na(per_b(nqx, S), per_b(nkx, S), per_b(nvx, S), per_b(nkc, C), per_b(nvc, C),
                 na_rpb[l], g_na_q[l], g_na_k[l])
        xs = _mix_out(xs, mx, 5, ya.reshape(B * S, -1), yb.reshape(B * S, -1), ux, lw, S)
        xs = _ffn(xs, mx, (6, 7, 8), ffn_w, l, 1)
        if not last:
            ya_c = _mla_attention(per_b(qc, C), [kv_c], bound)
            yb_c = _small_attn(per_b(nqc, C), per_b(nkc, C), per_b(nvc, C))
            xc = _mix_out(xc, mc, 5, ya_c.reshape(B * C, -1), yb_c.reshape(B * C, -1), uc, lw, C)
            xc = _ffn(xc, mc, (6, 7, 8), ffn_w, l, 1)
    return xs.reshape(B, S, D)
```

```python
import functools
import math

import numpy as np
import jax
import jax.numpy as jnp
from jax import lax
from jax.experimental import pallas as pl
from jax.experimental.pallas import tpu as pltpu

F32 = jnp.float32
BF16 = jnp.bfloat16

LANES = 128
SUBLANES = 8

GRID_W = 64
MLA_HEADS = 8
MLA_NOPE = 64
MLA_ROPE = 32
MLA_QK = MLA_NOPE + MLA_ROPE
MLA_V = 64
MLA_Q_RANK = 768
MLA_KV_RANK = 256
MLA_WIDTH = MLA_HEADS * MLA_V
MLA_SCALE = 1.0 / math.sqrt(MLA_QK)
MLA_PAD = LANES
LOG2E = math.log2(math.e)
MLA_SCALE_LOG2 = MLA_SCALE * LOG2E
MLA_SHIFT_LANE = MLA_QK
MLA_FAST_BOUND = 60.0
MLA_KEY_TILE = 256
NA_HEADS = 4
NA_HEAD_DIM = 64
NA_WIDTH = NA_HEADS * NA_HEAD_DIM
NA_WIN_R = 8
NA_WIN_C = 16
NA_SCALE = 1.0 / math.sqrt(NA_HEAD_DIM)
NA_FAST_SPAN = 120.0
NA_GROUPS_PER_STEP = 4
NA_GROUP_ROWS = 4
NA_KEY_ROWS = NA_GROUP_ROWS + NA_WIN_R
POOL_WINDOWS = (2, 4, 8, 16)
POOL_GROUP_DIM = 64
POOL_WIDTH = len(POOL_WINDOWS) * POOL_GROUP_DIM
POOL_HALO = max(POOL_WINDOWS) // 2
FFN_CHUNK = 256
MIX_IN_ROWS = 512
MIX_IN_SUB_ROWS = 256
MIX_OUT_ROWS = 1024
MIX_OUT_SUB_ROWS = 256
N_MOD = 9
ROPE_BASE = 10000.0
EPS = 1e-6
NEG_BIG = -1e30

COL_CQ = 0
COL_CKV = COL_CQ + MLA_Q_RANK
COL_KR = COL_CKV + MLA_KV_RANK
COL_KRS = COL_KR + MLA_PAD
COL_NQ = COL_KRS + MLA_PAD
COL_NK = COL_NQ + NA_WIDTH
COL_NV = COL_NK + NA_WIDTH
COL_U = COL_NV + NA_WIDTH
IN_EXT = COL_U + POOL_WIDTH


def _dot(a, b):
    return jnp.dot(a, b, preferred_element_type=F32)


def _dot_t(a, b):
    return lax.dot_general(a, b, (((1,), (1,)), ((), ())), preferred_element_type=F32)


def _silu(x):
    return x * jax.nn.sigmoid(x)


def _cparams(sem, vmem_mb):
    return pltpu.CompilerParams(dimension_semantics=sem, vmem_limit_bytes=vmem_mb * 1024 * 1024)


def _mod_kernel(c_ref, w_ref, b_ref, o_ref):
    h = _silu(c_ref[...]).astype(BF16)
    o_ref[0] = _dot(h, w_ref[0].astype(BF16)) + b_ref[0]


def _modulation(cc, w_mod, b_mod):
    L, D, NM = w_mod.shape
    tn = 1024
    return pl.pallas_call(
        _mod_kernel,
        grid=(L, NM // tn),
        in_specs=[pl.BlockSpec((SUBLANES, D), lambda l, j: (0, 0)),
                  pl.BlockSpec((1, D, tn), lambda l, j: (l, 0, j)),
                  pl.BlockSpec((1, 1, tn), lambda l, j: (l, 0, j))],
        out_specs=pl.BlockSpec((1, SUBLANES, tn), lambda l, j: (l, 0, j)),
        out_shape=jax.ShapeDtypeStruct((L, SUBLANES, NM), F32),
        compiler_params=_cparams(("arbitrary", "arbitrary"), 32),
        name="modulation",
    )(cc, w_mod, b_mod.reshape(L, 1, NM))


def _modulated_norm(x, shift, scale):
    ms = jnp.mean(x * x, axis=-1, keepdims=True)
    return (x * lax.rsqrt(ms + EPS)) * (1.0 + scale) + shift


def _ffn_kernel(x_ref, sh_ref, sc_ref, gt_ref, wg_ref, wu_ref, wd_ref, o_ref):
    x = x_ref[...]
    h = _modulated_norm(x, sh_ref[0], sc_ref[0]).astype(BF16)
    acc = None
    for c in range(wg_ref.shape[1] // FFN_CHUNK):
        cols = slice(c * FFN_CHUNK, (c + 1) * FFN_CHUNK)
        g = _dot(h, wg_ref[:, cols])
        u = _dot(h, wu_ref[:, cols])
        y = _dot((_silu(g) * u).astype(BF16), wd_ref[cols, :])
        acc = y if acc is None else acc + y
    o_ref[...] = x + (0.5 * gt_ref[0]) * acc


def _resident(a):
    return pl.BlockSpec(a.shape, lambda i: (0,) * a.ndim, pipeline_mode=pl.Buffered(1))


def _mod_spec(mod, j, per):
    table, first, _ = mod
    return pl.BlockSpec((1, 1, table.shape[-1]), lambda i: (first + (i // per) * N_MOD + j, 0, 0))


def _ffn(x, mod, js, weights, layer, half):
    T, D = x.shape
    nb = mod[2]
    tm = min(512, T // nb)
    per = (T // nb) // tm
    whole = lambda a: pl.BlockSpec((None, None) + a.shape[2:], lambda i: (layer, half, 0, 0),
                                   pipeline_mode=pl.Buffered(1))
    return pl.pallas_call(
        _ffn_kernel,
        grid=(T // tm,),
        in_specs=[pl.BlockSpec((tm, D), lambda i: (i, 0))] + [_mod_spec(mod, j, per) for j in js]
                 + [whole(a) for a in weights],
        out_specs=pl.BlockSpec((tm, D), lambda i: (i, 0)),
        out_shape=jax.ShapeDtypeStruct((T, D), F32),
        compiler_params=_cparams(("arbitrary",), 48),
        name="ffn",
    )(x, mod[0], mod[0], mod[0], *weights)


def _gain_norm(x, gain, width):
    ms = jnp.sum(x * x, axis=-1, keepdims=True) * (1.0 / width)
    return x * lax.rsqrt(ms + EPS) * gain


def _head_rsqrt(sumsq):
    return lax.rsqrt(sumsq * (1.0 / MLA_QK) + EPS)


def _mla_score_bound(gq, gk):
    amax = lambda g: jnp.max(jnp.abs(g), axis=-1, keepdims=True)
    return (MLA_SCALE_LOG2 * MLA_QK) * amax(gq) * amax(gk)


def _group_mean_square(x, bd):
    xx = x * x
    hi = xx.astype(BF16)
    lo = (xx - hi.astype(F32)).astype(BF16)
    return _dot(hi, bd) + _dot(lo, bd)


def _pre_kernel(x_ref, sh_ref, sc_ref, win_ref, gcq_ref, wq_ref, gckv_ref, wkv_ref, gq_ref, gqs_ref, gk_ref, gks_ref,
                cos_ref, sin_ref, bd_ref, gnq_ref, gnk_ref,
                q_ref, k_ref, v_ref, nq_ref, nk_ref, nv_ref, u_ref, qf_scr, kvf_scr, zt_scr, *, sub_rows):
    lane = lax.broadcasted_iota(jnp.int32, (1, MLA_PAD), 1)
    real = lane < MLA_QK
    shift_lane = lane == MLA_SHIFT_LANE
    gq, gk = gq_ref[...], gk_ref[...]
    neg_bound = -_mla_score_bound(gq, gk)
    ones = jnp.ones((sub_rows, MLA_PAD), BF16)
    n_sub = x_ref.shape[0] // sub_rows

    @pl.when(pl.program_id(0) == 0)
    def _():
        qf_scr[...] = jnp.zeros_like(qf_scr)
        kvf_scr[...] = jnp.zeros_like(kvf_scr)
        zt_scr[...] = jnp.zeros_like(zt_scr)

    zt = lambda rows, lo, hi: zt_scr[rows, lo - COL_KR:hi - COL_KR]
    for s in range(n_sub):
        rows = slice(s * sub_rows, (s + 1) * sub_rows)
        cos, sin = cos_ref[rows, :], sin_ref[rows, :]
        qf = qf_scr[rows, :]
        q_cos = cos * (gq * MLA_SCALE_LOG2)
        q_sin = sin * (gqs_ref[...] * MLA_SCALE_LOG2)
        for hd in range(MLA_HEADS):
            sl = slice(hd * MLA_PAD, (hd + 1) * MLA_PAD)
            x = qf[:, sl]
            r = _head_rsqrt(jnp.sum(jnp.where(real, x * x, 0.0), axis=-1, keepdims=True))
            qh = (x * q_cos + pltpu.roll(x, MLA_PAD - MLA_ROPE, 1) * q_sin) * r
            q_ref[rows, sl] = jnp.where(shift_lane, neg_bound, qh).astype(BF16)

        kvf = kvf_scr[rows, :]
        kr = zt(rows, COL_KR, COL_KRS)
        k_rot = kr * (cos * gk) + zt(rows, COL_KRS, COL_NQ) * (sin * gks_ref[...])
        kr_sq = jnp.sum(kr * kr, axis=-1, keepdims=True)
        for hd in range(MLA_HEADS):
            sl = slice(hd * MLA_PAD, (hd + 1) * MLA_PAD)
            x = kvf[:, sl]
            r = _head_rsqrt(jnp.sum(x * x, axis=-1, keepdims=True) + kr_sq)
            k_ref[rows, sl] = jnp.where(shift_lane, 1.0, (x * gk + k_rot) * r).astype(BF16)
        for pr in range(MLA_HEADS // 2):
            vcols = slice(MLA_HEADS * MLA_PAD + pr * MLA_PAD, MLA_HEADS * MLA_PAD + (pr + 1) * MLA_PAD)
            v_ref[rows, 2 * pr * MLA_PAD:(2 * pr + 1) * MLA_PAD] = kvf[:, vcols].astype(BF16)
            v_ref[rows, (2 * pr + 1) * MLA_PAD:(2 * pr + 2) * MLA_PAD] = ones

        bd = bd_ref[...]
        nq = zt(rows, COL_NQ, COL_NK)
        nq_ref[rows, :] = (nq * lax.rsqrt(_group_mean_square(nq, bd) + EPS) * (gnq_ref[...] * (NA_SCALE * LOG2E))).astype(BF16)
        nk = zt(rows, COL_NK, COL_NV)
        nk_ref[rows, :] = (nk * lax.rsqrt(_group_mean_square(nk, bd) + EPS) * gnk_ref[...]).astype(BF16)
        nv_ref[rows, :] = zt(rows, COL_NV, COL_U).astype(BF16)
        u_ref[rows, :] = zt(rows, COL_U, IN_EXT)

    for s in range(n_sub):
        rows = slice(s * sub_rows, (s + 1) * sub_rows)
        h = _modulated_norm(x_ref[rows, :], sh_ref[0], sc_ref[0]).astype(BF16)
        z = _dot(h, win_ref[...])
        cq = _gain_norm(z[:, COL_CQ:COL_CKV], gcq_ref[...], MLA_Q_RANK).astype(BF16)
        qf_scr[rows, :] = _dot(cq, wq_ref[...])
        ckv = _gain_norm(z[:, COL_CKV:COL_KR], gckv_ref[...], MLA_KV_RANK).astype(BF16)
        kvf_scr[rows, :] = _dot(ckv, wkv_ref[...])
        zt_scr[rows, :] = z[:, COL_KR:]


def _pre(x, mod, js, lw, tables, seq_len):
    T, D = x.shape
    nb = mod[2]
    tm = min(MIX_IN_ROWS, seq_len)
    n = T // tm
    per = (T // nb) // tm
    per_seq = seq_len // tm
    cur = lambda i: jnp.minimum(i, n - 1)
    prv = lambda i: jnp.maximum(i - 1, 0)
    first = mod[1]
    mod_spec = lambda j: pl.BlockSpec((1, 1, D), lambda i: (first + (cur(i) // per) * N_MOD + j, 0, 0))
    out = lambda w: pl.BlockSpec((tm, w), lambda i: (prv(i), 0))
    full = _resident
    pos = pl.BlockSpec((tm, LANES), lambda i: (prv(i) % per_seq, 0))
    weights = (lw["w_in"], lw["g_cq"], lw["w_q"], lw["g_ckv"], lw["w_kv"],
               lw["g_q"], lw["g_q_partner"], lw["g_k"], lw["g_k_partner"])
    tail = (lw["bd"], lw["g_nq"], lw["g_nk"])
    hq = MLA_HEADS * MLA_PAD
    return pl.pallas_call(
        functools.partial(_pre_kernel, sub_rows=min(MIX_IN_SUB_ROWS, tm)),
        grid=(n + 1,),
        in_specs=[pl.BlockSpec((tm, D), lambda i: (cur(i), 0))] + [mod_spec(j) for j in js]
                 + [full(a) for a in weights] + [pos, pos] + [full(a) for a in tail],
        out_specs=[out(hq), out(hq), out(hq), out(NA_WIDTH), out(NA_WIDTH), out(NA_WIDTH), out(POOL_WIDTH)],
        out_shape=[jax.ShapeDtypeStruct((T, hq), BF16), jax.ShapeDtypeStruct((T, hq), BF16),
                   jax.ShapeDtypeStruct((T, hq), BF16), jax.ShapeDtypeStruct((T, NA_WIDTH), BF16),
                   jax.ShapeDtypeStruct((T, NA_WIDTH), BF16), jax.ShapeDtypeStruct((T, NA_WIDTH), BF16),
                   jax.ShapeDtypeStruct((T, POOL_WIDTH), F32)],
        scratch_shapes=[pltpu.VMEM((tm, hq), F32), pltpu.VMEM((tm, hq + MLA_WIDTH), F32),
                        pltpu.VMEM((tm, IN_EXT - COL_KR), F32)],
        compiler_params=_cparams(("arbitrary",), 56),
        name="mix_in",
    )(x, mod[0], mod[0], *weights, *tables, *tail)


def _flash_fast_kernel(q_ref, *refs):
    o_ref = refs[-1]
    tq = q_ref.shape[1]
    lanes = [slice(hd * MLA_PAD, (hd + 1) * MLA_PAD) for hd in range(2)]
    q = [q_ref[0, :, sl] for sl in lanes]
    acc = [jnp.zeros((tq, 2 * MLA_PAD), F32) for _ in range(2)]
    for k_ref, v_ref in zip(refs[0:-1:2], refs[1:-1:2]):
        for t in range(k_ref.shape[1] // MLA_KEY_TILE):
            keys = slice(t * MLA_KEY_TILE, (t + 1) * MLA_KEY_TILE)
            for hd in range(2):
                p = jnp.exp2(_dot_t(q[hd], k_ref[0, keys, lanes[hd]])).astype(BF16)
                acc[hd] = acc[hd] + _dot(p, v_ref[0, keys, :])
    for hd in range(2):
        own = slice(hd * MLA_V, (hd + 1) * MLA_V)
        o_ref[0, :, own] = (acc[hd][:, own] / acc[hd][:, MLA_PAD:MLA_PAD + 1]).astype(o_ref.dtype)


def _flash_fast(q, kv):
    B, Nq, HQ = q.shape
    pairs = HQ // (2 * MLA_PAD)
    tq = min(1024, Nq)
    kv_specs = [pl.BlockSpec((1, a.shape[1], 2 * MLA_PAD), lambda b, h, i: (b, 0, h)) for pair in kv for a in pair]
    return pl.pallas_call(
        _flash_fast_kernel,
        grid=(B, pairs, Nq // tq),
        in_specs=[pl.BlockSpec((1, tq, 2 * MLA_PAD), lambda b, h, i: (b, i, h))] + kv_specs,
        out_specs=pl.BlockSpec((1, tq, 2 * MLA_V), lambda b, h, i: (b, i, h)),
        out_shape=jax.ShapeDtypeStruct((B, Nq, pairs * 2 * MLA_V), BF16),
        compiler_params=_cparams(("arbitrary", "arbitrary", "arbitrary"), 56),
        name="mla_attention",
    )(q, *[a for pair in kv for a in pair])


def _flash_safe_kernel(q_ref, k_ref, v_ref, o_ref, *, tk):
    tq = q_ref.shape[1]
    nk = k_ref.shape[1] // tk
    for hd in range(2):
        lanes = slice(hd * MLA_PAD, (hd + 1) * MLA_PAD)
        q = q_ref[0, :, lanes]

        def body(j, carry):
            m, acc = carry
            start = pl.multiple_of(j * tk, tk)
            s = _dot_t(q, k_ref[0, pl.ds(start, tk), lanes])
            m_new = jnp.maximum(m, jnp.max(s, axis=-1, keepdims=True))
            p = jnp.exp2(s - m_new)
            acc = jnp.exp2(m - m_new) * acc + _dot(p.astype(BF16), v_ref[0, pl.ds(start, tk), :])
            return m_new, acc

        init = (jnp.full((tq, 1), NEG_BIG, F32), jnp.zeros((tq, 2 * MLA_PAD), F32))
        _, acc = lax.fori_loop(0, nk, body, init)
        own = slice(hd * MLA_V, (hd + 1) * MLA_V)
        o_ref[0, :, own] = (acc[:, own] / acc[:, MLA_PAD:MLA_PAD + 1]).astype(o_ref.dtype)


def _flash_safe(q, kv):
    k = jnp.concatenate([pair[0] for pair in kv], axis=1)
    v = jnp.concatenate([pair[1] for pair in kv], axis=1)
    B, Nq, HQ = q.shape
    Nk = k.shape[1]
    pairs = HQ // (2 * MLA_PAD)
    tq = min(512, Nq)
    tk = next(t for t in (768, 512, 256) if Nk % t == 0)
    kv_spec = pl.BlockSpec((1, Nk, 2 * MLA_PAD), lambda b, h, i: (b, 0, h))
    return pl.pallas_call(
        functools.partial(_flash_safe_kernel, tk=tk),
        grid=(B, pairs, Nq // tq),
        in_specs=[pl.BlockSpec((1, tq, 2 * MLA_PAD), lambda b, h, i: (b, i, h)), kv_spec, kv_spec],
        out_specs=pl.BlockSpec((1, tq, 2 * MLA_V), lambda b, h, i: (b, i, h)),
        out_shape=jax.ShapeDtypeStruct((B, Nq, pairs * 2 * MLA_V), BF16),
        compiler_params=_cparams(("arbitrary", "arbitrary", "arbitrary"), 56),
        name="mla_attention_running_max",
    )(q, k, v)


def _mla_attention(q, kv, bound):
    return lax.cond(bound <= MLA_FAST_BOUND, lambda: _flash_fast(q, kv), lambda: _flash_safe(q, kv))


def _head_masks(width, head_dim, heads):
    lane = lax.broadcasted_iota(jnp.int32, (1, width), 1)
    return [(lane >= h * head_dim) & (lane < (h + 1) * head_dim) for h in range(heads)]


def _na_kernel(q_ref, k_ref, v_ref, kc_ref, vc_ref, bias_ref, bound_ref, o_ref, *, running_max):
    rows = k_ref.shape[1] // GRID_W
    groups = rows // NA_GROUP_ROWS
    tq = NA_GROUP_ROWS * GRID_W
    kc, vc = kc_ref[0], vc_ref[0]
    bound = bound_ref[...]
    masks = _head_masks(NA_WIDTH, NA_HEAD_DIM, NA_HEADS)
    for gi in range(q_ref.shape[1] // tq):
        g = pl.program_id(1) * (q_ref.shape[1] // tq) + gi
        variant = jnp.where(g == 0, 0, jnp.where(g == groups - 1, 2, 1))
        base = jnp.clip(g * NA_GROUP_ROWS - NA_WIN_R // 2, 0, rows - NA_KEY_ROWS)
        start = pl.multiple_of(base * GRID_W, GRID_W)
        kw = k_ref[0, pl.ds(start, NA_KEY_ROWS * GRID_W), :]
        vw = v_ref[0, pl.ds(start, NA_KEY_ROWS * GRID_W), :]
        q = q_ref[0, gi * tq:(gi + 1) * tq, :]
        out = jnp.zeros(q.shape, F32)
        for hd, mask in enumerate(masks):
            qh = jnp.where(mask, q, jnp.zeros_like(q))
            s_loc = _dot_t(qh, kw) + bias_ref[variant, hd]
            s_ctx = _dot_t(qh, kc) - bound
            if running_max:
                m = jnp.maximum(jnp.max(s_loc, axis=-1, keepdims=True), jnp.max(s_ctx, axis=-1, keepdims=True))
                s_loc, s_ctx = s_loc - m, s_ctx - m
            p_loc = jnp.exp2(s_loc)
            p_ctx = jnp.exp2(s_ctx)
            l = jnp.sum(p_loc, axis=-1, keepdims=True) + jnp.sum(p_ctx, axis=-1, keepdims=True)
            o = _dot(p_loc.astype(BF16), vw) + _dot(p_ctx.astype(BF16), vc)
            out = jnp.where(mask, o / l, out)
        o_ref[0, gi * tq:(gi + 1) * tq, :] = out.astype(o_ref.dtype)


def _na_call(q, k, v, kc, vc, bias, bound, running_max):
    B, S, W = q.shape
    C = kc.shape[1]
    tq = NA_GROUP_ROWS * GRID_W * NA_GROUPS_PER_STEP
    return pl.pallas_call(
        functools.partial(_na_kernel, running_max=running_max),
        grid=(B, S // tq),
        in_specs=[pl.BlockSpec((1, tq, W), lambda b, g: (b, g, 0)),
                  pl.BlockSpec((1, S, W), lambda b, g: (b, 0, 0)),
                  pl.BlockSpec((1, S, W), lambda b, g: (b, 0, 0)),
                  pl.BlockSpec((1, C, W), lambda b, g: (b, 0, 0)),
                  pl.BlockSpec((1, C, W), lambda b, g: (b, 0, 0)),
                  pl.BlockSpec(bias.shape, lambda b, g: (0, 0, 0, 0), pipeline_mode=pl.Buffered(1)),
                  pl.BlockSpec((1, 1), lambda b, g: (0, 0))],
        out_specs=pl.BlockSpec((1, tq, W), lambda b, g: (b, g, 0)),
        out_shape=jax.ShapeDtypeStruct((B, S, W), BF16),
        compiler_params=_cparams(("arbitrary", "arbitrary"), 56),
        name="na_attention_running_max" if running_max else "na_attention",
    )(q, k, v, kc, vc, bias, bound.reshape(1, 1))


def _na(q, k, v, kc, vc, rpb, g_nq, g_nk):
    qk = (LOG2E * NA_SCALE * NA_HEAD_DIM) * jnp.max(jnp.abs(g_nq)) * jnp.max(jnp.abs(g_nk))
    hi = LOG2E * jnp.maximum(jnp.max(rpb), 0.0)
    lo = LOG2E * jnp.minimum(jnp.min(rpb), 0.0)
    bound = qk + hi
    bias = _na_bias_tables(rpb, bound)
    args = (q, k, v, kc, vc, bias, bound)
    return lax.cond(2.0 * qk + hi - lo <= NA_FAST_SPAN,
                    lambda: _na_call(*args, running_max=False), lambda: _na_call(*args, running_max=True))


def _na_bias_tables(rpb, bound):
    n_c = 2 * NA_WIN_C - 1
    w = np.arange(GRID_W)[:, None]
    kcol = np.arange(GRID_W)[None, :]
    cs = np.clip(w - NA_WIN_C // 2, 0, GRID_W - NA_WIN_C)
    col_ok = (kcol >= cs) & (kcol < cs + NA_WIN_C)
    col_sel = (col_ok[..., None] & ((kcol - w + NA_WIN_C - 1)[..., None] == np.arange(n_c))).astype(np.float32)
    blocks = jnp.einsum("hrc,wkc->hrwk", rpb, col_sel, precision=lax.Precision.HIGHEST)
    blocks = jnp.where(col_ok, LOG2E * blocks - bound, NEG_BIG)
    masked = jnp.full((rpb.shape[0], GRID_W, GRID_W), NEG_BIG, F32)
    half = NA_WIN_R // 2
    variants = []
    for v in range(3):
        group = []
        for i in range(NA_GROUP_ROWS):
            q_rel, r0 = ((i, 0), (half + i, i), (NA_WIN_R + i, half))[v]
            row = [blocks[:, j - q_rel + NA_WIN_R - 1] if r0 <= j < r0 + NA_WIN_R else masked
                   for j in range(NA_KEY_ROWS)]
            group.append(jnp.concatenate(row, axis=-1))
        variants.append(jnp.concatenate(group, axis=1))
    return jnp.stack(variants).astype(F32)


def _small_attn_kernel(q_ref, k_ref, v_ref, o_ref):
    q, k, v = q_ref[0], k_ref[0], v_ref[0]
    out = jnp.zeros(q.shape, F32)
    for mask in _head_masks(NA_WIDTH, NA_HEAD_DIM, NA_HEADS):
        s = _dot_t(jnp.where(mask, q, jnp.zeros_like(q)), k)
        p = jnp.exp2(s - jnp.max(s, axis=-1, keepdims=True))
        o = _dot(p.astype(BF16), v)
        out = jnp.where(mask, o / jnp.sum(p, axis=-1, keepdims=True), out)
    o_ref[0] = out.astype(o_ref.dtype)


def _small_attn(q, k, v):
    B, N, W = q.shape
    spec = pl.BlockSpec((1, N, W), lambda b: (b, 0, 0))
    return pl.pallas_call(
        _small_attn_kernel, grid=(B,), in_specs=[spec, spec, spec], out_specs=spec,
        out_shape=jax.ShapeDtypeStruct((B, N, W), BF16),
        compiler_params=_cparams(("arbitrary",), 32),
        name="ctx_na_attention",
    )(q, k, v)


def _mix_out_kernel(x_ref, gt_ref, ya_ref, yb_ref, u_ref, up_ref, un_ref, wo_ref, wp_ref, ps_ref, o_ref,
                    *, per_seq, seq_len):
    tm = x_ref.shape[0]
    i = pl.program_id(0) % per_seq
    prev = jnp.where(i == 0, 0.0, up_ref[...])
    nxt = jnp.where(i == per_seq - 1, 0.0, un_ref[...])
    ext = jnp.concatenate([prev, u_ref[...], nxt], axis=0)
    lane = lax.broadcasted_iota(jnp.int32, (1, POOL_WIDTH), 1)
    grp = lane // POOL_GROUP_DIM
    half = jnp.where(grp == 0, POOL_WINDOWS[0] // 2,
                     jnp.where(grp == 1, POOL_WINDOWS[1] // 2,
                               jnp.where(grp == 2, POOL_WINDOWS[2] // 2, POOL_WINDOWS[3] // 2)))
    sub = min(MIX_OUT_SUB_ROWS, tm)
    n = sub + 2 * POOL_HALO
    for s in range(tm // sub):
        rows = slice(s * sub, (s + 1) * sub)
        e = ext[s * sub:s * sub + n]
        mid = lambda w: w[POOL_HALO:POOL_HALO + sub]
        halves = []
        for c in range(2):
            ec = e[:, c * LANES:(c + 1) * LANES]
            p = ec + pltpu.roll(ec, 1, 0)
            levels = [p]
            for step in (1, 2, 4)[:2 * c + 1]:
                p = pltpu.roll(p, step, 0) + pltpu.roll(p, n - step, 0)
                levels.append(p)
            halves.append(jnp.where(lane[:, :LANES] < POOL_GROUP_DIM, mid(levels[-2]), mid(levels[-1])))
        win = jnp.concatenate(halves, axis=1)
        t = i * tm + s * sub + lax.broadcasted_iota(jnp.int32, (sub, 1), 0)
        cnt = jnp.minimum(t + half, seq_len) - jnp.maximum(t - half, 0)
        pooled = win / cnt.astype(F32) - mid(e)
        yc = _dot(pooled.astype(BF16), wp_ref[...]) * ps_ref[...]
        y = _dot(jnp.concatenate([ya_ref[rows, :], yb_ref[rows, :], yc.astype(BF16)], axis=1), wo_ref[...])
        o_ref[rows, :] = x_ref[rows, :] + gt_ref[0] * y


def _mix_out(x, mod, j_gate, ya, yb, u, lw, seq_len):
    T, D = x.shape
    nb = mod[2]
    tm = min(MIX_OUT_ROWS, seq_len)
    per = (T // nb) // tm
    per_seq = seq_len // tm
    hb = tm // POOL_HALO
    last_hb = T // POOL_HALO - 1
    tok = lambda w: pl.BlockSpec((tm, w), lambda i: (i, 0))
    full = _resident
    return pl.pallas_call(
        functools.partial(_mix_out_kernel, per_seq=per_seq, seq_len=seq_len),
        grid=(T // tm,),
        in_specs=[tok(D), _mod_spec(mod, j_gate, per),
                  tok(MLA_WIDTH), tok(NA_WIDTH), tok(POOL_WIDTH),
                  pl.BlockSpec((POOL_HALO, POOL_WIDTH), lambda i: (jnp.maximum(i * hb - 1, 0), 0)),
                  pl.BlockSpec((POOL_HALO, POOL_WIDTH), lambda i: (jnp.minimum((i + 1) * hb, last_hb), 0)),
                  full(lw["w_out"]), full(lw["w_pool"]), full(lw["pool_scale"])],
        out_specs=tok(D),
        out_shape=jax.ShapeDtypeStruct((T, D), F32),
        compiler_params=_cparams(("arbitrary",), 48),
        name="mix_out",
    )(x, mod[0], ya, yb, u, u, u, lw["w_out"], lw["w_pool"], lw["pool_scale"])


def _rope_partner(a):
    lead = a.shape[:-1]
    return a.reshape(lead + (2, 2, MLA_ROPE // 4))[..., ::-1, :].reshape(lead + (MLA_ROPE,))


def _rope_tables(n_tok):
    t = np.arange(n_tok)
    axis_dim = MLA_ROPE // 2
    inv_freq = ROPE_BASE ** (-np.arange(0, axis_dim, 2, dtype=np.float64) / axis_dim)
    ar = (t // GRID_W)[:, None] * inv_freq
    ac = (t % GRID_W)[:, None] * inv_freq
    ones = np.ones((n_tok, MLA_NOPE))
    zeros = np.zeros((n_tok, MLA_NOPE))
    pad = np.zeros((n_tok, MLA_PAD - MLA_QK))
    cos = np.concatenate([ones, np.cos(ar), np.cos(ar), np.cos(ac), np.cos(ac), pad + 1.0], axis=1)
    sin = np.concatenate([zeros, -np.sin(ar), np.sin(ar), -np.sin(ac), np.sin(ac), pad], axis=1)
    return jnp.asarray(cos, F32), jnp.asarray(sin, F32)


def _identity_tables(n_tok):
    return jnp.ones((n_tok, MLA_PAD), F32), jnp.zeros((n_tok, MLA_PAD), F32)


def _layer_weights(l, w_in, g_cq, w_q_up, g_ckv, w_kv_up, g_mla_q, g_mla_k, g_na_q, g_na_k,
                   w_pool, pool_scale, w_out):
    D = w_in.shape[1]
    wi = w_in[l]
    w_kr = wi[:, COL_KR:COL_KR + MLA_ROPE]
    in_rope_lanes = lambda w: jnp.pad(w, ((0, 0), (MLA_NOPE, MLA_PAD - MLA_QK)))
    w_in_ext = jnp.concatenate([wi[:, :COL_KR], in_rope_lanes(w_kr), in_rope_lanes(_rope_partner(w_kr)),
                                wi[:, COL_KR + MLA_ROPE:]], axis=1).astype(BF16)
    wq = w_q_up[l].reshape(MLA_Q_RANK, MLA_HEADS, MLA_QK)
    wq = jnp.concatenate([wq, _rope_partner(wq[:, :, MLA_NOPE:])], axis=2).reshape(MLA_Q_RANK, MLA_HEADS * MLA_PAD)
    wkv = w_kv_up[l].reshape(MLA_KV_RANK, MLA_HEADS, MLA_NOPE + MLA_V)
    wk = jnp.pad(wkv[:, :, :MLA_NOPE], ((0, 0), (0, 0), (0, MLA_PAD - MLA_NOPE)))
    wv = wkv[:, :, MLA_NOPE:]
    w_kv = jnp.concatenate([wk.reshape(MLA_KV_RANK, -1), wv.reshape(MLA_KV_RANK, -1)], axis=1)
    pad_gain = lambda g: jnp.pad(g, (0, MLA_PAD - MLA_QK)).reshape(1, MLA_PAD)
    partner_gain = lambda g: jnp.pad(_rope_partner(g[MLA_NOPE:]), (MLA_NOPE, MLA_PAD - MLA_QK)).reshape(1, MLA_PAD)
    head_id = np.arange(NA_WIDTH) // NA_HEAD_DIM
    bd = jnp.asarray((head_id[:, None] == head_id[None, :]) / NA_HEAD_DIM, BF16)
    wp = jnp.zeros((POOL_WIDTH, POOL_WIDTH), F32)
    for g in range(len(POOL_WINDOWS)):
        sl = slice(g * POOL_GROUP_DIM, (g + 1) * POOL_GROUP_DIM)
        wp = wp.at[sl, sl].set(w_pool[l, g])
    return {
        "w_in": w_in_ext, "g_cq": g_cq[l].reshape(1, -1), "w_q": wq.astype(BF16),
        "g_ckv": g_ckv[l].reshape(1, -1), "w_kv": w_kv.astype(BF16),
        "g_q": pad_gain(g_mla_q[l]), "g_k": pad_gain(g_mla_k[l]),
        "g_q_partner": partner_gain(g_mla_q[l]), "g_k_partner": partner_gain(g_mla_k[l]),
        "bd": bd, "g_nq": jnp.tile(g_na_q[l], NA_HEADS).reshape(1, -1), "g_nk": jnp.tile(g_na_k[l], NA_HEADS).reshape(1, -1),
        "w_pool": wp.astype(BF16), "pool_scale": pool_scale[l].reshape(1, -1), "w_out": w_out[l].astype(BF16),
    }


def kernel(x, c, ctx, c_ctx, w_mod, b_mod, w_ffn_gate, w_ffn_up, w_ffn_down, w_in, g_cq, w_q_up, g_ckv, w_kv_up,
           g_mla_q, g_mla_k, g_na_q, g_na_k, na_rpb, w_pool, pool_scale, w_out):
    B, S, D = x.shape
    C = ctx.shape[1]
    L = w_mod.shape[0]
    assert S % (NA_GROUPS_PER_STEP * NA_GROUP_ROWS * GRID_W) == 0 and S // GRID_W >= NA_KEY_ROWS and B + 1 <= SUBLANES

    cc = jnp.zeros((SUBLANES, D), F32).at[:B].set(c).at[B].set(c_ctx)
    mods = _modulation(cc, w_mod, b_mod).reshape(L * SUBLANES * N_MOD, 1, D)
    rope = _rope_tables(S)
    no_rope = _identity_tables(C)
    ffn_w = (w_ffn_gate.astype(BF16), w_ffn_up.astype(BF16), w_ffn_down.astype(BF16))

    xs = x.reshape(B * S, D)
    xc = ctx.reshape(B * C, D)
    for l in range(L):
        last = l == L - 1
        mx = (mods, l * SUBLANES * N_MOD, B)
        mc = (mods, (l * SUBLANES + B) * N_MOD, 1)
        lw = _layer_weights(l, w_in, g_cq, w_q_up, g_ckv, w_kv_up, g_mla_q, g_mla_k, g_na_q, g_na_k,
                            w_pool, pool_scale, w_out)
        xs = _ffn(xs, mx, (0, 1, 2), ffn_w, l, 0)
        xc = _ffn(xc, mc, (0, 1, 2), ffn_w, l, 0)

        qx, kx, vx, nqx, nkx, nvx, ux = _pre(xs, mx, (3, 4), lw, rope, S)
        qc, kc, vc, nqc, nkc, nvc, uc = _pre(xc, mc, (3, 4), lw, no_rope, C)
        per_b = lambda a, n: a.reshape(B, n, a.shape[-1])
        bound = _mla_score_bound(lw["g_q"], lw["g_k"])[0, 0]
        kv_c = (per_b(kc, C), per_b(vc, C))
        ya = _mla_attention(per_b(qx, S), [kv_c, (per_b(kx, S), per_b(vx, S))], bound)
        yb = _na(per_b(nqx, S), per_b(nkx, S), per_b(nvx, S), per_b(nkc, C), per_b(nvc, C),
                 na_rpb[l], g_na_q[l], g_na_k[l])
        xs = _mix_out(xs, mx, 5, ya.reshape(B * S, -1), yb.reshape(B * S, -1), ux, lw, S)
        xs = _ffn(xs, mx, (6, 7, 8), ffn_w, l, 1)
        if not last:
            ya_c = _mla_attention(per_b(qc, C), [kv_c], bound)
            yb_c = _small_attn(per_b(nqc, C), per_b(nkc, C), per_b(nvc, C))
            xc = _mix_out(xc, mc, 5, ya_c.reshape(B * C, -1), yb_c.reshape(B * C, -1), uc, lw, C)
            xc = _ffn(xc, mc, (6, 7, 8), ffn_w, l, 1)
    return xs.reshape(B, S, D)
```

```python
import functools
import math

import numpy as np
import jax
import jax.numpy as jnp
from jax import lax
from jax.experimental import pallas as pl
from jax.experimental.pallas import tpu as pltpu

F32 = jnp.float32
BF16 = jnp.bfloat16

LANES = 128
SUBLANES = 8
VMEM_V7X_MIB = 64
VMEM_SMALL_MIB = 32
VMEM_MEDIUM_MIB = 48
VMEM_LARGE_MIB = 56

GRID_W = 64
MLA_HEADS = 8
MLA_NOPE = 64
MLA_ROPE = 32
MLA_QK = MLA_NOPE + MLA_ROPE
MLA_V = 64
MLA_Q_RANK = 768
MLA_KV_RANK = 256
MLA_WIDTH = MLA_HEADS * MLA_V
MLA_SCALE = 1.0 / math.sqrt(MLA_QK)
MLA_PAD = LANES
LOG2E = math.log2(math.e)
MLA_SCALE_LOG2 = MLA_SCALE * LOG2E
MLA_SHIFT_LANE = MLA_QK
MLA_FAST_BOUND = 60.0
MLA_KEY_TILE = 256
NA_HEADS = 4
NA_HEAD_DIM = 64
NA_WIDTH = NA_HEADS * NA_HEAD_DIM
NA_WIN_R = 8
NA_WIN_C = 16
NA_SCALE = 1.0 / math.sqrt(NA_HEAD_DIM)
NA_FAST_SPAN = 120.0
NA_GROUPS_PER_STEP = 4
NA_GROUP_ROWS = 4
NA_KEY_ROWS = NA_GROUP_ROWS + NA_WIN_R
POOL_WINDOWS = (2, 4, 8, 16)
POOL_GROUP_DIM = 64
POOL_WIDTH = len(POOL_WINDOWS) * POOL_GROUP_DIM
POOL_HALO = max(POOL_WINDOWS) // 2
MOD_COLS = 1024
FFN_ROWS = 512
FFN_CHUNK = 256
MLA_Q_ROWS = 1024
MIX_IN_ROWS = 512
MIX_IN_SUB_ROWS = 256
MIX_OUT_ROWS = 1024
MIX_OUT_SUB_ROWS = 256
N_MOD = 9
ROPE_BASE = 10000.0
EPS = 1e-6
NEG_BIG = -1e30

COL_CQ = 0
COL_CKV = COL_CQ + MLA_Q_RANK
COL_KR = COL_CKV + MLA_KV_RANK
COL_KRS = COL_KR + MLA_PAD
COL_NQ = COL_KRS + MLA_PAD
COL_NK = COL_NQ + NA_WIDTH
COL_NV = COL_NK + NA_WIDTH
COL_U = COL_NV + NA_WIDTH
IN_EXT = COL_U + POOL_WIDTH


def _dot(a, b):
    return jnp.dot(a, b, preferred_element_type=F32)


def _dot_t(a, b):
    return lax.dot_general(a, b, (((1,), (1,)), ((), ())), preferred_element_type=F32)


def _silu(x):
    return x * jax.nn.sigmoid(x)


def _cparams(n_axes, vmem_mib):
    assert vmem_mib <= VMEM_V7X_MIB
    return pltpu.CompilerParams(dimension_semantics=("arbitrary",) * n_axes, vmem_limit_bytes=vmem_mib * 1024 * 1024)


def _mod_kernel(c_ref, w_ref, b_ref, o_ref):
    h = _silu(c_ref[...]).astype(BF16)
    o_ref[0] = _dot(h, w_ref[0].astype(BF16)) + b_ref[0]


def _modulation(cc, w_mod, b_mod):
    L, D, NM = w_mod.shape
    tn = MOD_COLS
    return pl.pallas_call(
        _mod_kernel,
        grid=(L, NM // tn),
        in_specs=[pl.BlockSpec((SUBLANES, D), lambda l, j: (0, 0)),
                  pl.BlockSpec((1, D, tn), lambda l, j: (l, 0, j)),
                  pl.BlockSpec((1, 1, tn), lambda l, j: (l, 0, j))],
        out_specs=pl.BlockSpec((1, SUBLANES, tn), lambda l, j: (l, 0, j)),
        out_shape=jax.ShapeDtypeStruct((L, SUBLANES, NM), F32),
        compiler_params=_cparams(2, VMEM_SMALL_MIB),
        name="modulation",
    )(cc, w_mod, b_mod.reshape(L, 1, NM))


def _modulated_norm(x, shift, scale):
    ms = jnp.mean(x * x, axis=-1, keepdims=True)
    return (x * lax.rsqrt(ms + EPS)) * (1.0 + scale) + shift


def _ffn_kernel(x_ref, sh_ref, sc_ref, gt_ref, wg_ref, wu_ref, wd_ref, o_ref):
    x = x_ref[...]
    h = _modulated_norm(x, sh_ref[0], sc_ref[0]).astype(BF16)
    acc = None
    for c in range(wg_ref.shape[1] // FFN_CHUNK):
        cols = slice(c * FFN_CHUNK, (c + 1) * FFN_CHUNK)
        g = _dot(h, wg_ref[:, cols])
        u = _dot(h, wu_ref[:, cols])
        y = _dot((_silu(g) * u).astype(BF16), wd_ref[cols, :])
        acc = y if acc is None else acc + y
    o_ref[...] = x + (0.5 * gt_ref[0]) * acc


def _resident(a):
    return pl.BlockSpec(a.shape, lambda i: (0,) * a.ndim, pipeline_mode=pl.Buffered(1))


def _mod_spec(mod, j, per):
    table, first, _ = mod
    return pl.BlockSpec((1, 1, table.shape[-1]), lambda i: (first + (i // per) * N_MOD + j, 0, 0))


def _ffn(x, mod, js, weights, layer, half):
    T, D = x.shape
    nb = mod[2]
    tm = min(FFN_ROWS, T // nb)
    per = (T // nb) // tm
    whole = lambda a: pl.BlockSpec((None, None) + a.shape[2:], lambda i: (layer, half, 0, 0),
                                   pipeline_mode=pl.Buffered(1))
    return pl.pallas_call(
        _ffn_kernel,
        grid=(T // tm,),
        in_specs=[pl.BlockSpec((tm, D), lambda i: (i, 0))] + [_mod_spec(mod, j, per) for j in js]
                 + [whole(a) for a in weights],
        out_specs=pl.BlockSpec((tm, D), lambda i: (i, 0)),
        out_shape=jax.ShapeDtypeStruct((T, D), F32),
        compiler_params=_cparams(1, VMEM_MEDIUM_MIB),
        name="ffn",
    )(x, mod[0], mod[0], mod[0], *weights)


def _gain_norm(x, gain, width):
    ms = jnp.sum(x * x, axis=-1, keepdims=True) * (1.0 / width)
    return x * lax.rsqrt(ms + EPS) * gain


def _head_rsqrt(sumsq):
    return lax.rsqrt(sumsq * (1.0 / MLA_QK) + EPS)


def _mla_score_bound(gq, gk):
    amax = lambda g: jnp.max(jnp.abs(g), axis=-1, keepdims=True)
    return (MLA_SCALE_LOG2 * MLA_QK) * amax(gq) * amax(gk)


def _group_mean_square(x, bd):
    xx = x * x
    hi = xx.astype(BF16)
    lo = (xx - hi.astype(F32)).astype(BF16)
    return _dot(hi, bd) + _dot(lo, bd)


def _pre_kernel(x_ref, sh_ref, sc_ref, win_ref, gcq_ref, wq_ref, gckv_ref, wkv_ref, gq_ref, gqs_ref, gk_ref, gks_ref,
                cos_ref, sin_ref, bd_ref, gnq_ref, gnk_ref,
                q_ref, k_ref, v_ref, nq_ref, nk_ref, nv_ref, u_ref, qf_scr, kvf_scr, zt_scr, *, sub_rows):
    lane = lax.broadcasted_iota(jnp.int32, (1, MLA_PAD), 1)
    real = lane < MLA_QK
    shift_lane = lane == MLA_SHIFT_LANE
    gq, gk = gq_ref[...], gk_ref[...]
    neg_bound = -_mla_score_bound(gq, gk)
    ones = jnp.ones((sub_rows, MLA_PAD), BF16)
    n_sub = x_ref.shape[0] // sub_rows

    @pl.when(pl.program_id(0) == 0)
    def _():
        qf_scr[...] = jnp.zeros_like(qf_scr)
        kvf_scr[...] = jnp.zeros_like(kvf_scr)
        zt_scr[...] = jnp.zeros_like(zt_scr)

    zt = lambda rows, lo, hi: zt_scr[rows, lo - COL_KR:hi - COL_KR]
    for s in range(n_sub):
        rows = slice(s * sub_rows, (s + 1) * sub_rows)
        cos, sin = cos_ref[rows, :], sin_ref[rows, :]
        qf = qf_scr[rows, :]
        q_cos = cos * (gq * MLA_SCALE_LOG2)
        q_sin = sin * (gqs_ref[...] * MLA_SCALE_LOG2)
        for hd in range(MLA_HEADS):
            sl = slice(hd * MLA_PAD, (hd + 1) * MLA_PAD)
            x = qf[:, sl]
            r = _head_rsqrt(jnp.sum(jnp.where(real, x * x, 0.0), axis=-1, keepdims=True))
            qh = (x * q_cos + pltpu.roll(x, MLA_PAD - MLA_ROPE, 1) * q_sin) * r
            q_ref[rows, sl] = jnp.where(shift_lane, neg_bound, qh).astype(BF16)

        kvf = kvf_scr[rows, :]
        kr = zt(rows, COL_KR, COL_KRS)
        k_rot = kr * (cos * gk) + zt(rows, COL_KRS, COL_NQ) * (sin * gks_ref[...])
        kr_sq = jnp.sum(kr * kr, axis=-1, keepdims=True)
        for hd in range(MLA_HEADS):
            sl = slice(hd * MLA_PAD, (hd + 1) * MLA_PAD)
            x = kvf[:, sl]
            r = _head_rsqrt(jnp.sum(x * x, axis=-1, keepdims=True) + kr_sq)
            k_ref[rows, sl] = jnp.where(shift_lane, 1.0, (x * gk + k_rot) * r).astype(BF16)
        for pr in range(MLA_HEADS // 2):
            vcols = slice(MLA_HEADS * MLA_PAD + pr * MLA_PAD, MLA_HEADS * MLA_PAD + (pr + 1) * MLA_PAD)
            v_ref[rows, 2 * pr * MLA_PAD:(2 * pr + 1) * MLA_PAD] = kvf[:, vcols].astype(BF16)
            v_ref[rows, (2 * pr + 1) * MLA_PAD:(2 * pr + 2) * MLA_PAD] = ones

        bd = bd_ref[...]
        nq = zt(rows, COL_NQ, COL_NK)
        nq_ref[rows, :] = (nq * lax.rsqrt(_group_mean_square(nq, bd) + EPS) * (gnq_ref[...] * (NA_SCALE * LOG2E))).astype(BF16)
        nk = zt(rows, COL_NK, COL_NV)
        nk_ref[rows, :] = (nk * lax.rsqrt(_group_mean_square(nk, bd) + EPS) * gnk_ref[...]).astype(BF16)
        nv_ref[rows, :] = zt(rows, COL_NV, COL_U).astype(BF16)
        u_ref[rows, :] = zt(rows, COL_U, IN_EXT)

    for s in range(n_sub):
        rows = slice(s * sub_rows, (s + 1) * sub_rows)
        h = _modulated_norm(x_ref[rows, :], sh_ref[0], sc_ref[0]).astype(BF16)
        z = _dot(h, win_ref[...])
        cq = _gain_norm(z[:, COL_CQ:COL_CKV], gcq_ref[...], MLA_Q_RANK).astype(BF16)
        qf_scr[rows, :] = _dot(cq, wq_ref[...])
        ckv = _gain_norm(z[:, COL_CKV:COL_KR], gckv_ref[...], MLA_KV_RANK).astype(BF16)
        kvf_scr[rows, :] = _dot(ckv, wkv_ref[...])
        zt_scr[rows, :] = z[:, COL_KR:]


def _pre(x, mod, js, lw, tables, seq_len):
    T, D = x.shape
    nb = mod[2]
    tm = min(MIX_IN_ROWS, seq_len)
    n = T // tm
    per = (T // nb) // tm
    per_seq = seq_len // tm
    cur = lambda i: jnp.minimum(i, n - 1)
    prv = lambda i: jnp.maximum(i - 1, 0)
    first = mod[1]
    mod_spec = lambda j: pl.BlockSpec((1, 1, D), lambda i: (first + (cur(i) // per) * N_MOD + j, 0, 0))
    out = lambda w: pl.BlockSpec((tm, w), lambda i: (prv(i), 0))
    full = _resident
    pos = pl.BlockSpec((tm, LANES), lambda i: (prv(i) % per_seq, 0))
    weights = (lw["w_in"], lw["g_cq"], lw["w_q"], lw["g_ckv"], lw["w_kv"],
               lw["g_q"], lw["g_q_partner"], lw["g_k"], lw["g_k_partner"])
    tail = (lw["bd"], lw["g_nq"], lw["g_nk"])
    hq = MLA_HEADS * MLA_PAD
    return pl.pallas_call(
        functools.partial(_pre_kernel, sub_rows=min(MIX_IN_SUB_ROWS, tm)),
        grid=(n + 1,),
        in_specs=[pl.BlockSpec((tm, D), lambda i: (cur(i), 0))] + [mod_spec(j) for j in js]
                 + [full(a) for a in weights] + [pos, pos] + [full(a) for a in tail],
        out_specs=[out(hq), out(hq), out(hq), out(NA_WIDTH), out(NA_WIDTH), out(NA_WIDTH), out(POOL_WIDTH)],
        out_shape=[jax.ShapeDtypeStruct((T, hq), BF16), jax.ShapeDtypeStruct((T, hq), BF16),
                   jax.ShapeDtypeStruct((T, hq), BF16), jax.ShapeDtypeStruct((T, NA_WIDTH), BF16),
                   jax.ShapeDtypeStruct((T, NA_WIDTH), BF16), jax.ShapeDtypeStruct((T, NA_WIDTH), BF16),
                   jax.ShapeDtypeStruct((T, POOL_WIDTH), F32)],
        scratch_shapes=[pltpu.VMEM((tm, hq), F32), pltpu.VMEM((tm, hq + MLA_WIDTH), F32),
                        pltpu.VMEM((tm, IN_EXT - COL_KR), F32)],
        compiler_params=_cparams(1, VMEM_LARGE_MIB),
        name="mix_in",
    )(x, mod[0], mod[0], *weights, *tables, *tail)


def _flash_fast_kernel(q_ref, *refs):
    o_ref = refs[-1]
    tq = q_ref.shape[1]
    lanes = [slice(hd * MLA_PAD, (hd + 1) * MLA_PAD) for hd in range(2)]
    q = [q_ref[0, :, sl] for sl in lanes]
    acc = [jnp.zeros((tq, 2 * MLA_PAD), F32) for _ in range(2)]
    for k_ref, v_ref in zip(refs[0:-1:2], refs[1:-1:2]):
        for t in range(k_ref.shape[1] // MLA_KEY_TILE):
            keys = slice(t * MLA_KEY_TILE, (t + 1) * MLA_KEY_TILE)
            for hd in range(2):
                p = jnp.exp2(_dot_t(q[hd], k_ref[0, keys, lanes[hd]])).astype(BF16)
                acc[hd] = acc[hd] + _dot(p, v_ref[0, keys, :])
    for hd in range(2):
        own = slice(hd * MLA_V, (hd + 1) * MLA_V)
        o_ref[0, :, own] = (acc[hd][:, own] / acc[hd][:, MLA_PAD:MLA_PAD + 1]).astype(o_ref.dtype)


def _flash_fast(q, kv):
    B, Nq, HQ = q.shape
    pairs = HQ // (2 * MLA_PAD)
    tq = min(MLA_Q_ROWS, Nq)
    kv_specs = [pl.BlockSpec((1, a.shape[1], 2 * MLA_PAD), lambda b, h, i: (b, 0, h)) for pair in kv for a in pair]
    return pl.pallas_call(
        _flash_fast_kernel,
        grid=(B, pairs, Nq // tq),
        in_specs=[pl.BlockSpec((1, tq, 2 * MLA_PAD), lambda b, h, i: (b, i, h))] + kv_specs,
        out_specs=pl.BlockSpec((1, tq, 2 * MLA_V), lambda b, h, i: (b, i, h)),
        out_shape=jax.ShapeDtypeStruct((B, Nq, pairs * 2 * MLA_V), BF16),
        compiler_params=_cparams(3, VMEM_LARGE_MIB),
        name="mla_attention",
    )(q, *[a for pair in kv for a in pair])


def _flash_safe_kernel(q_ref, k_ref, v_ref, o_ref, *, tk):
    tq = q_ref.shape[1]
    nk = k_ref.shape[1] // tk
    for hd in range(2):
        lanes = slice(hd * MLA_PAD, (hd + 1) * MLA_PAD)
        q = q_ref[0, :, lanes]

        def body(j, carry):
            m, acc = carry
            start = pl.multiple_of(j * tk, tk)
            s = _dot_t(q, k_ref[0, pl.ds(start, tk), lanes])
            m_new = jnp.maximum(m, jnp.max(s, axis=-1, keepdims=True))
            p = jnp.exp2(s - m_new)
            acc = jnp.exp2(m - m_new) * acc + _dot(p.astype(BF16), v_ref[0, pl.ds(start, tk), :])
            return m_new, acc

        init = (jnp.full((tq, 1), NEG_BIG, F32), jnp.zeros((tq, 2 * MLA_PAD), F32))
        _, acc = lax.fori_loop(0, nk, body, init)
        own = slice(hd * MLA_V, (hd + 1) * MLA_V)
        o_ref[0, :, own] = (acc[:, own] / acc[:, MLA_PAD:MLA_PAD + 1]).astype(o_ref.dtype)


def _flash_safe(q, kv):
    k = jnp.concatenate([pair[0] for pair in kv], axis=1)
    v = jnp.concatenate([pair[1] for pair in kv], axis=1)
    B, Nq, HQ = q.shape
    Nk = k.shape[1]
    pairs = HQ // (2 * MLA_PAD)
    tq = min(MLA_Q_ROWS // 2, Nq)
    tk = next(t for t in (3 * MLA_KEY_TILE, 2 * MLA_KEY_TILE, MLA_KEY_TILE) if Nk % t == 0)
    kv_spec = pl.BlockSpec((1, Nk, 2 * MLA_PAD), lambda b, h, i: (b, 0, h))
    return pl.pallas_call(
        functools.partial(_flash_safe_kernel, tk=tk),
        grid=(B, pairs, Nq // tq),
        in_specs=[pl.BlockSpec((1, tq, 2 * MLA_PAD), lambda b, h, i: (b, i, h)), kv_spec, kv_spec],
        out_specs=pl.BlockSpec((1, tq, 2 * MLA_V), lambda b, h, i: (b, i, h)),
        out_shape=jax.ShapeDtypeStruct((B, Nq, pairs * 2 * MLA_V), BF16),
        compiler_params=_cparams(3, VMEM_LARGE_MIB),
        name="mla_attention_running_max",
    )(q, k, v)


def _mla_attention(q, kv, bound):
    return lax.cond(bound <= MLA_FAST_BOUND, lambda: _flash_fast(q, kv), lambda: _flash_safe(q, kv))


def _head_masks(width, head_dim, heads):
    lane = lax.broadcasted_iota(jnp.int32, (1, width), 1)
    return [(lane >= h * head_dim) & (lane < (h + 1) * head_dim) for h in range(heads)]


def _na_kernel(q_ref, k_ref, v_ref, kc_ref, vc_ref, bias_ref, bound_ref, o_ref, *, running_max):
    rows = k_ref.shape[1] // GRID_W
    groups = rows // NA_GROUP_ROWS
    tq = NA_GROUP_ROWS * GRID_W
    kc, vc = kc_ref[0], vc_ref[0]
    bound = bound_ref[...]
    masks = _head_masks(NA_WIDTH, NA_HEAD_DIM, NA_HEADS)
    for gi in range(q_ref.shape[1] // tq):
        g = pl.program_id(1) * (q_ref.shape[1] // tq) + gi
        variant = jnp.where(g == 0, 0, jnp.where(g == groups - 1, 2, 1))
        base = jnp.clip(g * NA_GROUP_ROWS - NA_WIN_R // 2, 0, rows - NA_KEY_ROWS)
        start = pl.multiple_of(base * GRID_W, GRID_W)
        kw = k_ref[0, pl.ds(start, NA_KEY_ROWS * GRID_W), :]
        vw = v_ref[0, pl.ds(start, NA_KEY_ROWS * GRID_W), :]
        q = q_ref[0, gi * tq:(gi + 1) * tq, :]
        out = jnp.zeros(q.shape, F32)
        for hd, mask in enumerate(masks):
            qh = jnp.where(mask, q, jnp.zeros_like(q))
            s_loc = _dot_t(qh, kw) + bias_ref[variant, hd]
            s_ctx = _dot_t(qh, kc) - bound
            if running_max:
                m = jnp.maximum(jnp.max(s_loc, axis=-1, keepdims=True), jnp.max(s_ctx, axis=-1, keepdims=True))
                s_loc, s_ctx = s_loc - m, s_ctx - m
            p_loc = jnp.exp2(s_loc)
            p_ctx = jnp.exp2(s_ctx)
            l = jnp.sum(p_loc, axis=-1, keepdims=True) + jnp.sum(p_ctx, axis=-1, keepdims=True)
            o = _dot(p_loc.astype(BF16), vw) + _dot(p_ctx.astype(BF16), vc)
            out = jnp.where(mask, o / l, out)
        o_ref[0, gi * tq:(gi + 1) * tq, :] = out.astype(o_ref.dtype)


def _na_call(q, k, v, kc, vc, bias, bound, running_max):
    B, S, W = q.shape
    C = kc.shape[1]
    tq = NA_GROUP_ROWS * GRID_W * NA_GROUPS_PER_STEP
    return pl.pallas_call(
        functools.partial(_na_kernel, running_max=running_max),
        grid=(B, S // tq),
        in_specs=[pl.BlockSpec((1, tq, W), lambda b, g: (b, g, 0)),
                  pl.BlockSpec((1, S, W), lambda b, g: (b, 0, 0)),
                  pl.BlockSpec((1, S, W), lambda b, g: (b, 0, 0)),
                  pl.BlockSpec((1, C, W), lambda b, g: (b, 0, 0)),
                  pl.BlockSpec((1, C, W), lambda b, g: (b, 0, 0)),
                  pl.BlockSpec(bias.shape, lambda b, g: (0, 0, 0, 0), pipeline_mode=pl.Buffered(1)),
                  pl.BlockSpec((1, 1), lambda b, g: (0, 0))],
        out_specs=pl.BlockSpec((1, tq, W), lambda b, g: (b, g, 0)),
        out_shape=jax.ShapeDtypeStruct((B, S, W), BF16),
        compiler_params=_cparams(2, VMEM_LARGE_MIB),
        name="na_attention_running_max" if running_max else "na_attention",
    )(q, k, v, kc, vc, bias, bound.reshape(1, 1))


def _na(q, k, v, kc, vc, rpb, g_nq, g_nk):
    qk = (LOG2E * NA_SCALE * NA_HEAD_DIM) * jnp.max(jnp.abs(g_nq)) * jnp.max(jnp.abs(g_nk))
    hi = LOG2E * jnp.maximum(jnp.max(rpb), 0.0)
    lo = LOG2E * jnp.minimum(jnp.min(rpb), 0.0)
    bound = qk + hi
    bias = _na_bias_tables(rpb, bound)
    args = (q, k, v, kc, vc, bias, bound)
    return lax.cond(2.0 * qk + hi - lo <= NA_FAST_SPAN,
                    lambda: _na_call(*args, running_max=False), lambda: _na_call(*args, running_max=True))


def _na_bias_tables(rpb, bound):
    n_c = 2 * NA_WIN_C - 1
    w = np.arange(GRID_W)[:, None]
    kcol = np.arange(GRID_W)[None, :]
    cs = np.clip(w - NA_WIN_C // 2, 0, GRID_W - NA_WIN_C)
    col_ok = (kcol >= cs) & (kcol < cs + NA_WIN_C)
    col_sel = (col_ok[..., None] & ((kcol - w + NA_WIN_C - 1)[..., None] == np.arange(n_c))).astype(np.float32)
    blocks = jnp.einsum("hrc,wkc->hrwk", rpb, col_sel, precision=lax.Precision.HIGHEST)
    blocks = jnp.where(col_ok, LOG2E * blocks - bound, NEG_BIG)
    masked = jnp.full((rpb.shape[0], GRID_W, GRID_W), NEG_BIG, F32)
    half = NA_WIN_R // 2
    variants = []
    for v in range(3):
        group = []
        for i in range(NA_GROUP_ROWS):
            q_rel, r0 = ((i, 0), (half + i, i), (NA_WIN_R + i, half))[v]
            row = [blocks[:, j - q_rel + NA_WIN_R - 1] if r0 <= j < r0 + NA_WIN_R else masked
                   for j in range(NA_KEY_ROWS)]
            group.append(jnp.concatenate(row, axis=-1))
        variants.append(jnp.concatenate(group, axis=1))
    return jnp.stack(variants).astype(F32)


def _small_attn_kernel(q_ref, k_ref, v_ref, o_ref):
    q, k, v = q_ref[0], k_ref[0], v_ref[0]
    out = jnp.zeros(q.shape, F32)
    for mask in _head_masks(NA_WIDTH, NA_HEAD_DIM, NA_HEADS):
        s = _dot_t(jnp.where(mask, q, jnp.zeros_like(q)), k)
        p = jnp.exp2(s - jnp.max(s, axis=-1, keepdims=True))
        o = _dot(p.astype(BF16), v)
        out = jnp.where(mask, o / jnp.sum(p, axis=-1, keepdims=True), out)
    o_ref[0] = out.astype(o_ref.dtype)


def _small_attn(q, k, v):
    B, N, W = q.shape
    spec = pl.BlockSpec((1, N, W), lambda b: (b, 0, 0))
    return pl.pallas_call(
        _small_attn_kernel, grid=(B,), in_specs=[spec, spec, spec], out_specs=spec,
        out_shape=jax.ShapeDtypeStruct((B, N, W), BF16),
        compiler_params=_cparams(1, VMEM_SMALL_MIB),
        name="ctx_na_attention",
    )(q, k, v)


def _mix_out_kernel(x_ref, gt_ref, ya_ref, yb_ref, u_ref, up_ref, un_ref, wo_ref, wp_ref, ps_ref, o_ref,
                    *, per_seq, seq_len):
    tm = x_ref.shape[0]
    i = pl.program_id(0) % per_seq
    prev = jnp.where(i == 0, 0.0, up_ref[...])
    nxt = jnp.where(i == per_seq - 1, 0.0, un_ref[...])
    ext = jnp.concatenate([prev, u_ref[...], nxt], axis=0)
    lane = lax.broadcasted_iota(jnp.int32, (1, POOL_WIDTH), 1)
    grp = lane // POOL_GROUP_DIM
    half = jnp.where(grp == 0, POOL_WINDOWS[0] // 2,
                     jnp.where(grp == 1, POOL_WINDOWS[1] // 2,
                               jnp.where(grp == 2, POOL_WINDOWS[2] // 2, POOL_WINDOWS[3] // 2)))
    sub = min(MIX_OUT_SUB_ROWS, tm)
    n = sub + 2 * POOL_HALO
    for s in range(tm // sub):
        rows = slice(s * sub, (s + 1) * sub)
        e = ext[s * sub:s * sub + n]
        mid = lambda w: w[POOL_HALO:POOL_HALO + sub]
        halves = []
        for c in range(2):
            ec = e[:, c * LANES:(c + 1) * LANES]
            p = ec + pltpu.roll(ec, 1, 0)
            levels = [p]
            for step in (1, 2, 4)[:2 * c + 1]:
                p = pltpu.roll(p, step, 0) + pltpu.roll(p, n - step, 0)
                levels.append(p)
            halves.append(jnp.where(lane[:, :LANES] < POOL_GROUP_DIM, mid(levels[-2]), mid(levels[-1])))
        win = jnp.concatenate(halves, axis=1)
        t = i * tm + s * sub + lax.broadcasted_iota(jnp.int32, (sub, 1), 0)
        cnt = jnp.minimum(t + half, seq_len) - jnp.maximum(t - half, 0)
        pooled = win / cnt.astype(F32) - mid(e)
        yc = _dot(pooled.astype(BF16), wp_ref[...]) * ps_ref[...]
        y = _dot(jnp.concatenate([ya_ref[rows, :], yb_ref[rows, :], yc.astype(BF16)], axis=1), wo_ref[...])
        o_ref[rows, :] = x_ref[rows, :] + gt_ref[0] * y


def _mix_out(x, mod, j_gate, ya, yb, u, lw, seq_len):
    T, D = x.shape
    nb = mod[2]
    tm = min(MIX_OUT_ROWS, seq_len)
    per = (T // nb) // tm
    per_seq = seq_len // tm
    hb = tm // POOL_HALO
    last_hb = T // POOL_HALO - 1
    tok = lambda w: pl.BlockSpec((tm, w), lambda i: (i, 0))
    full = _resident
    return pl.pallas_call(
        functools.partial(_mix_out_kernel, per_seq=per_seq, seq_len=seq_len),
        grid=(T // tm,),
        in_specs=[tok(D), _mod_spec(mod, j_gate, per),
                  tok(MLA_WIDTH), tok(NA_WIDTH), tok(POOL_WIDTH),
                  pl.BlockSpec((POOL_HALO, POOL_WIDTH), lambda i: (jnp.maximum(i * hb - 1, 0), 0)),
                  pl.BlockSpec((POOL_HALO, POOL_WIDTH), lambda i: (jnp.minimum((i + 1) * hb, last_hb), 0)),
                  full(lw["w_out"]), full(lw["w_pool"]), full(lw["pool_scale"])],
        out_specs=tok(D),
        out_shape=jax.ShapeDtypeStruct((T, D), F32),
        compiler_params=_cparams(1, VMEM_MEDIUM_MIB),
        name="mix_out",
    )(x, mod[0], ya, yb, u, u, u, lw["w_out"], lw["w_pool"], lw["pool_scale"])


def _rope_partner(a):
    lead = a.shape[:-1]
    return a.reshape(lead + (2, 2, MLA_ROPE // 4))[..., ::-1, :].reshape(lead + (MLA_ROPE,))


def _rope_tables(n_tok):
    t = np.arange(n_tok)
    axis_dim = MLA_ROPE // 2
    inv_freq = ROPE_BASE ** (-np.arange(0, axis_dim, 2, dtype=np.float64) / axis_dim)
    ar = (t // GRID_W)[:, None] * inv_freq
    ac = (t % GRID_W)[:, None] * inv_freq
    ones = np.ones((n_tok, MLA_NOPE))
    zeros = np.zeros((n_tok, MLA_NOPE))
    pad = np.zeros((n_tok, MLA_PAD - MLA_QK))
    cos = np.concatenate([ones, np.cos(ar), np.cos(ar), np.cos(ac), np.cos(ac), pad + 1.0], axis=1)
    sin = np.concatenate([zeros, -np.sin(ar), np.sin(ar), -np.sin(ac), np.sin(ac), pad], axis=1)
    return jnp.asarray(cos, F32), jnp.asarray(sin, F32)


def _identity_tables(n_tok):
    return jnp.ones((n_tok, MLA_PAD), F32), jnp.zeros((n_tok, MLA_PAD), F32)


def _layer_weights(l, w_in, g_cq, w_q_up, g_ckv, w_kv_up, g_mla_q, g_mla_k, g_na_q, g_na_k,
                   w_pool, pool_scale, w_out):
    D = w_in.shape[1]
    wi = w_in[l]
    w_kr = wi[:, COL_KR:COL_KR + MLA_ROPE]
    in_rope_lanes = lambda w: jnp.pad(w, ((0, 0), (MLA_NOPE, MLA_PAD - MLA_QK)))
    w_in_ext = jnp.concatenate([wi[:, :COL_KR], in_rope_lanes(w_kr), in_rope_lanes(_rope_partner(w_kr)),
                                wi[:, COL_KR + MLA_ROPE:]], axis=1).astype(BF16)
    wq = w_q_up[l].reshape(MLA_Q_RANK, MLA_HEADS, MLA_QK)
    wq = jnp.concatenate([wq, _rope_partner(wq[:, :, MLA_NOPE:])], axis=2).reshape(MLA_Q_RANK, MLA_HEADS * MLA_PAD)
    wkv = w_kv_up[l].reshape(MLA_KV_RANK, MLA_HEADS, MLA_NOPE + MLA_V)
    wk = jnp.pad(wkv[:, :, :MLA_NOPE], ((0, 0), (0, 0), (0, MLA_PAD - MLA_NOPE)))
    wv = wkv[:, :, MLA_NOPE:]
    w_kv = jnp.concatenate([wk.reshape(MLA_KV_RANK, -1), wv.reshape(MLA_KV_RANK, -1)], axis=1)
    pad_gain = lambda g: jnp.pad(g, (0, MLA_PAD - MLA_QK)).reshape(1, MLA_PAD)
    partner_gain = lambda g: jnp.pad(_rope_partner(g[MLA_NOPE:]), (MLA_NOPE, MLA_PAD - MLA_QK)).reshape(1, MLA_PAD)
    head_id = np.arange(NA_WIDTH) // NA_HEAD_DIM
    bd = jnp.asarray((head_id[:, None] == head_id[None, :]) / NA_HEAD_DIM, BF16)
    wp = jnp.zeros((POOL_WIDTH, POOL_WIDTH), F32)
    for g in range(len(POOL_WINDOWS)):
        sl = slice(g * POOL_GROUP_DIM, (g + 1) * POOL_GROUP_DIM)
        wp = wp.at[sl, sl].set(w_pool[l, g])
    return {
        "w_in": w_in_ext, "g_cq": g_cq[l].reshape(1, -1), "w_q": wq.astype(BF16),
        "g_ckv": g_ckv[l].reshape(1, -1), "w_kv": w_kv.astype(BF16),
        "g_q": pad_gain(g_mla_q[l]), "g_k": pad_gain(g_mla_k[l]),
        "g_q_partner": partner_gain(g_mla_q[l]), "g_k_partner": partner_gain(g_mla_k[l]),
        "bd": bd, "g_nq": jnp.tile(g_na_q[l], NA_HEADS).reshape(1, -1), "g_nk": jnp.tile(g_na_k[l], NA_HEADS).reshape(1, -1),
        "w_pool": wp.astype(BF16), "pool_scale": pool_scale[l].reshape(1, -1), "w_out": w_out[l].astype(BF16),
    }


def kernel(x, c, ctx, c_ctx, w_mod, b_mod, w_ffn_gate, w_ffn_up, w_ffn_down, w_in, g_cq, w_q_up, g_ckv, w_kv_up,
           g_mla_q, g_mla_k, g_na_q, g_na_k, na_rpb, w_pool, pool_scale, w_out):
    B, S, D = x.shape
    C = ctx.shape[1]
    L = w_mod.shape[0]
    assert S % (NA_GROUPS_PER_STEP * NA_GROUP_ROWS * GRID_W) == 0 and S // GRID_W >= NA_KEY_ROWS and B + 1 <= SUBLANES

    cc = jnp.zeros((SUBLANES, D), F32).at[:B].set(c).at[B].set(c_ctx)
    mods = _modulation(cc, w_mod, b_mod).reshape(L * SUBLANES * N_MOD, 1, D)
    rope = _rope_tables(S)
    no_rope = _identity_tables(C)
    ffn_w = (w_ffn_gate.astype(BF16), w_ffn_up.astype(BF16), w_ffn_down.astype(BF16))

    xs = x.reshape(B * S, D)
    xc = ctx.reshape(B * C, D)
    for l in range(L):
        last = l == L - 1
        mx = (mods, l * SUBLANES * N_MOD, B)
        mc = (mods, (l * SUBLANES + B) * N_MOD, 1)
        lw = _layer_weights(l, w_in, g_cq, w_q_up, g_ckv, w_kv_up, g_mla_q, g_mla_k, g_na_q, g_na_k,
                            w_pool, pool_scale, w_out)
        xs = _ffn(xs, mx, (0, 1, 2), ffn_w, l, 0)
        xc = _ffn(xc, mc, (0, 1, 2), ffn_w, l, 0)

        qx, kx, vx, nqx, nkx, nvx, ux = _pre(xs, mx, (3, 4), lw, rope, S)
        qc, kc, vc, nqc, nkc, nvc, uc = _pre(xc, mc, (3, 4), lw, no_rope, C)
        per_b = lambda a, n: a.reshape(B, n, a.shape[-1])
        bound = _mla_score_bound(lw["g_q"], lw["g_k"])[0, 0]
        kv_c = (per_b(kc, C), per_b(vc, C))
        ya = _mla_attention(per_b(qx, S), [kv_c, (per_b(kx, S), per_b(vx, S))], bound)
        yb = _na(per_b(nqx, S), per_b(nkx, S), per_b(nvx, S), per_b(nkc, C), per_b(nvc, C),
                 na_rpb[l], g_na_q[l], g_na_k[l])
        xs = _mix_out(xs, mx, 5, ya.reshape(B * S, -1), yb.reshape(B * S, -1), ux, lw, S)
        xs = _ffn(xs, mx, (6, 7, 8), ffn_w, l, 1)
        if not last:
            ya_c = _mla_attention(per_b(qc, C), [kv_c], bound)
            yb_c = _small_attn(per_b(nqc, C), per_b(nkc, C), per_b(nvc, C))
            xc = _mix_out(xc, mc, 5, ya_c.reshape(B * C, -1), yb_c.reshape(B * C, -1), uc, lw, C)
            xc = _ffn(xc, mc, (6, 7, 8), ffn_w, l, 1)
    return xs.reshape(B, S, D)
```

```python
import functools
import math

import numpy as np
import jax
import jax.numpy as jnp
from jax import lax
from jax.experimental import pallas as pl
from jax.experimental.pallas import tpu as pltpu

F32 = jnp.float32
BF16 = jnp.bfloat16

LANES = 128
SUBLANES = 8
VMEM_V7X_MIB = 64
VMEM_SMALL_MIB = 32
VMEM_MEDIUM_MIB = 48
VMEM_LARGE_MIB = 56

GRID_W = 64
MLA_HEADS = 8
MLA_NOPE = 64
MLA_ROPE = 32
MLA_QK = MLA_NOPE + MLA_ROPE
MLA_V = 64
MLA_Q_RANK = 768
MLA_KV_RANK = 256
MLA_WIDTH = MLA_HEADS * MLA_V
MLA_SCALE = 1.0 / math.sqrt(MLA_QK)
MLA_PAD = LANES
LOG2E = math.log2(math.e)
MLA_SCALE_LOG2 = MLA_SCALE * LOG2E
MLA_SHIFT_LANE = MLA_QK
MLA_FAST_BOUND = 60.0
MLA_KEY_TILE = 256
NA_HEADS = 4
NA_HEAD_DIM = 64
NA_WIDTH = NA_HEADS * NA_HEAD_DIM
NA_WIN_R = 8
NA_WIN_C = 16
NA_SCALE = 1.0 / math.sqrt(NA_HEAD_DIM)
NA_FAST_SPAN = 120.0
NA_GROUPS_PER_STEP = 8
NA_GROUP_ROWS = 4
NA_KEY_ROWS = NA_GROUP_ROWS + NA_WIN_R
POOL_WINDOWS = (2, 4, 8, 16)
POOL_GROUP_DIM = 64
POOL_WIDTH = len(POOL_WINDOWS) * POOL_GROUP_DIM
POOL_HALO = max(POOL_WINDOWS) // 2
MOD_COLS = 1024
FFN_ROWS = 512
FFN_CHUNK = 256
MLA_Q_ROWS = 1024
MIX_IN_ROWS = 512
MIX_IN_SUB_ROWS = 512
MIX_OUT_ROWS = 1024
MIX_OUT_SUB_ROWS = 256
N_MOD = 9
ROPE_BASE = 10000.0
EPS = 1e-6
NEG_BIG = -1e30

COL_CQ = 0
COL_CKV = COL_CQ + MLA_Q_RANK
COL_KR = COL_CKV + MLA_KV_RANK
COL_KRS = COL_KR + MLA_PAD
COL_NQ = COL_KRS + MLA_PAD
COL_NK = COL_NQ + NA_WIDTH
COL_NV = COL_NK + NA_WIDTH
COL_U = COL_NV + NA_WIDTH
IN_EXT = COL_U + POOL_WIDTH


def _dot(a, b):
    return jnp.dot(a, b, preferred_element_type=F32)


def _dot_t(a, b):
    return lax.dot_general(a, b, (((1,), (1,)), ((), ())), preferred_element_type=F32)


def _silu(x):
    return x * jax.nn.sigmoid(x)


def _cparams(n_axes, vmem_mib):
    assert vmem_mib <= VMEM_V7X_MIB
    return pltpu.CompilerParams(dimension_semantics=("arbitrary",) * n_axes, vmem_limit_bytes=vmem_mib * 1024 * 1024)


def _mod_kernel(c_ref, w_ref, b_ref, o_ref):
    h = _silu(c_ref[...]).astype(BF16)
    o_ref[0] = _dot(h, w_ref[0].astype(BF16)) + b_ref[0]


def _modulation(cc, w_mod, b_mod):
    L, D, NM = w_mod.shape
    tn = MOD_COLS
    return pl.pallas_call(
        _mod_kernel,
        grid=(L, NM // tn),
        in_specs=[pl.BlockSpec((SUBLANES, D), lambda l, j: (0, 0)),
                  pl.BlockSpec((1, D, tn), lambda l, j: (l, 0, j)),
                  pl.BlockSpec((1, 1, tn), lambda l, j: (l, 0, j))],
        out_specs=pl.BlockSpec((1, SUBLANES, tn), lambda l, j: (l, 0, j)),
        out_shape=jax.ShapeDtypeStruct((L, SUBLANES, NM), F32),
        compiler_params=_cparams(2, VMEM_SMALL_MIB),
        name="modulation",
    )(cc, w_mod, b_mod.reshape(L, 1, NM))


def _modulated_norm(x, shift, scale):
    ms = jnp.mean(x * x, axis=-1, keepdims=True)
    return (x * lax.rsqrt(ms + EPS)) * (1.0 + scale) + shift


def _ffn_kernel(x_ref, sh_ref, sc_ref, gt_ref, wg_ref, wu_ref, wd_ref, o_ref):
    x = x_ref[...]
    h = _modulated_norm(x, sh_ref[0], sc_ref[0]).astype(BF16)
    acc = None
    for c in range(wg_ref.shape[1] // FFN_CHUNK):
        cols = slice(c * FFN_CHUNK, (c + 1) * FFN_CHUNK)
        g = _dot(h, wg_ref[:, cols])
        u = _dot(h, wu_ref[:, cols])
        y = _dot((_silu(g) * u).astype(BF16), wd_ref[cols, :])
        acc = y if acc is None else acc + y
    o_ref[...] = x + (0.5 * gt_ref[0]) * acc


def _resident(a):
    return pl.BlockSpec(a.shape, lambda i: (0,) * a.ndim, pipeline_mode=pl.Buffered(1))


def _mod_spec(mod, j, per):
    table, first, _ = mod
    return pl.BlockSpec((1, 1, table.shape[-1]), lambda i: (first + (i // per) * N_MOD + j, 0, 0))


def _ffn(x, mod, js, weights, layer, half):
    T, D = x.shape
    nb = mod[2]
    tm = min(FFN_ROWS, T // nb)
    per = (T // nb) // tm
    whole = lambda a: pl.BlockSpec((None, None) + a.shape[2:], lambda i: (layer, half, 0, 0),
                                   pipeline_mode=pl.Buffered(1))
    return pl.pallas_call(
        _ffn_kernel,
        grid=(T // tm,),
        in_specs=[pl.BlockSpec((tm, D), lambda i: (i, 0))] + [_mod_spec(mod, j, per) for j in js]
                 + [whole(a) for a in weights],
        out_specs=pl.BlockSpec((tm, D), lambda i: (i, 0)),
        out_shape=jax.ShapeDtypeStruct((T, D), F32),
        compiler_params=_cparams(1, VMEM_MEDIUM_MIB),
        name="ffn",
    )(x, mod[0], mod[0], mod[0], *weights)


def _gain_norm(x, gain, width):
    ms = jnp.sum(x * x, axis=-1, keepdims=True) * (1.0 / width)
    return x * lax.rsqrt(ms + EPS) * gain


def _head_rsqrt(sumsq):
    return lax.rsqrt(sumsq * (1.0 / MLA_QK) + EPS)


def _mla_score_bound(gq, gk):
    amax = lambda g: jnp.max(jnp.abs(g), axis=-1, keepdims=True)
    return (MLA_SCALE_LOG2 * MLA_QK) * amax(gq) * amax(gk)


def _group_mean_square(x, bd):
    xx = x * x
    hi = xx.astype(BF16)
    lo = (xx - hi.astype(F32)).astype(BF16)
    return _dot(hi, bd) + _dot(lo, bd)


def _pre_kernel(x_ref, sh_ref, sc_ref, win_ref, gcq_ref, wq_ref, gckv_ref, wkv_ref, gq_ref, gqs_ref, gk_ref, gks_ref,
                cos_ref, sin_ref, bd_ref, gnq_ref, gnk_ref,
                q_ref, k_ref, v_ref, nq_ref, nk_ref, nv_ref, u_ref, qf_scr, kvf_scr, zt_scr, *, sub_rows):
    lane = lax.broadcasted_iota(jnp.int32, (1, MLA_PAD), 1)
    real = lane < MLA_QK
    shift_lane = lane == MLA_SHIFT_LANE
    gq, gk = gq_ref[...], gk_ref[...]
    neg_bound = -_mla_score_bound(gq, gk)
    ones = jnp.ones((sub_rows, MLA_PAD), BF16)
    n_sub = x_ref.shape[0] // sub_rows

    @pl.when(pl.program_id(0) == 0)
    def _():
        qf_scr[...] = jnp.zeros_like(qf_scr)
        kvf_scr[...] = jnp.zeros_like(kvf_scr)
        zt_scr[...] = jnp.zeros_like(zt_scr)

    zt = lambda rows, lo, hi: zt_scr[rows, lo - COL_KR:hi - COL_KR]
    for s in range(n_sub):
        rows = slice(s * sub_rows, (s + 1) * sub_rows)
        cos, sin = cos_ref[rows, :], sin_ref[rows, :]
        qf = qf_scr[rows, :]
        q_cos = cos * (gq * MLA_SCALE_LOG2)
        q_sin = sin * (gqs_ref[...] * MLA_SCALE_LOG2)
        for hd in range(MLA_HEADS):
            sl = slice(hd * MLA_PAD, (hd + 1) * MLA_PAD)
            x = qf[:, sl]
            r = _head_rsqrt(jnp.sum(jnp.where(real, x * x, 0.0), axis=-1, keepdims=True))
            qh = (x * q_cos + pltpu.roll(x, MLA_PAD - MLA_ROPE, 1) * q_sin) * r
            q_ref[rows, sl] = jnp.where(shift_lane, neg_bound, qh).astype(BF16)

        kvf = kvf_scr[rows, :]
        kr = zt(rows, COL_KR, COL_KRS)
        k_rot = kr * (cos * gk) + zt(rows, COL_KRS, COL_NQ) * (sin * gks_ref[...])
        kr_sq = jnp.sum(kr * kr, axis=-1, keepdims=True)
        for hd in range(MLA_HEADS):
            sl = slice(hd * MLA_PAD, (hd + 1) * MLA_PAD)
            x = kvf[:, sl]
            r = _head_rsqrt(jnp.sum(x * x, axis=-1, keepdims=True) + kr_sq)
            k_ref[rows, sl] = jnp.where(shift_lane, 1.0, (x * gk + k_rot) * r).astype(BF16)
        for pr in range(MLA_HEADS // 2):
            vcols = slice(MLA_HEADS * MLA_PAD + pr * MLA_PAD, MLA_HEADS * MLA_PAD + (pr + 1) * MLA_PAD)
            v_ref[rows, 2 * pr * MLA_PAD:(2 * pr + 1) * MLA_PAD] = kvf[:, vcols].astype(BF16)
            v_ref[rows, (2 * pr + 1) * MLA_PAD:(2 * pr + 2) * MLA_PAD] = ones

        bd = bd_ref[...]
        nq = zt(rows, COL_NQ, COL_NK)
        nq_ref[rows, :] = (nq * lax.rsqrt(_group_mean_square(nq, bd) + EPS) * (gnq_ref[...] * (NA_SCALE * LOG2E))).astype(BF16)
        nk = zt(rows, COL_NK, COL_NV)
        nk_ref[rows, :] = (nk * lax.rsqrt(_group_mean_square(nk, bd) + EPS) * gnk_ref[...]).astype(BF16)
        nv_ref[rows, :] = zt(rows, COL_NV, COL_U).astype(BF16)
        u_ref[rows, :] = zt(rows, COL_U, IN_EXT)

    for s in range(n_sub):
        rows = slice(s * sub_rows, (s + 1) * sub_rows)
        h = _modulated_norm(x_ref[rows, :], sh_ref[0], sc_ref[0]).astype(BF16)
        z = _dot(h, win_ref[...])
        cq = _gain_norm(z[:, COL_CQ:COL_CKV], gcq_ref[...], MLA_Q_RANK).astype(BF16)
        qf_scr[rows, :] = _dot(cq, wq_ref[...])
        ckv = _gain_norm(z[:, COL_CKV:COL_KR], gckv_ref[...], MLA_KV_RANK).astype(BF16)
        kvf_scr[rows, :] = _dot(ckv, wkv_ref[...])
        zt_scr[rows, :] = z[:, COL_KR:]


def _pre(x, mod, js, lw, tables, seq_len):
    T, D = x.shape
    nb = mod[2]
    tm = min(MIX_IN_ROWS, seq_len)
    n = T // tm
    per = (T // nb) // tm
    per_seq = seq_len // tm
    cur = lambda i: jnp.minimum(i, n - 1)
    prv = lambda i: jnp.maximum(i - 1, 0)
    first = mod[1]
    mod_spec = lambda j: pl.BlockSpec((1, 1, D), lambda i: (first + (cur(i) // per) * N_MOD + j, 0, 0))
    out = lambda w: pl.BlockSpec((tm, w), lambda i: (prv(i), 0))
    full = _resident
    pos = pl.BlockSpec((tm, LANES), lambda i: (prv(i) % per_seq, 0))
    weights = (lw["w_in"], lw["g_cq"], lw["w_q"], lw["g_ckv"], lw["w_kv"],
               lw["g_q"], lw["g_q_partner"], lw["g_k"], lw["g_k_partner"])
    tail = (lw["bd"], lw["g_nq"], lw["g_nk"])
    hq = MLA_HEADS * MLA_PAD
    return pl.pallas_call(
        functools.partial(_pre_kernel, sub_rows=min(MIX_IN_SUB_ROWS, tm)),
        grid=(n + 1,),
        in_specs=[pl.BlockSpec((tm, D), lambda i: (cur(i), 0))] + [mod_spec(j) for j in js]
                 + [full(a) for a in weights] + [pos, pos] + [full(a) for a in tail],
        out_specs=[out(hq), out(hq), out(hq), out(NA_WIDTH), out(NA_WIDTH), out(NA_WIDTH), out(POOL_WIDTH)],
        out_shape=[jax.ShapeDtypeStruct((T, hq), BF16), jax.ShapeDtypeStruct((T, hq), BF16),
                   jax.ShapeDtypeStruct((T, hq), BF16), jax.ShapeDtypeStruct((T, NA_WIDTH), BF16),
                   jax.ShapeDtypeStruct((T, NA_WIDTH), BF16), jax.ShapeDtypeStruct((T, NA_WIDTH), BF16),
                   jax.ShapeDtypeStruct((T, POOL_WIDTH), F32)],
        scratch_shapes=[pltpu.VMEM((tm, hq), F32), pltpu.VMEM((tm, hq + MLA_WIDTH), F32),
                        pltpu.VMEM((tm, IN_EXT - COL_KR), F32)],
        compiler_params=_cparams(1, VMEM_LARGE_MIB),
        name="mix_in",
    )(x, mod[0], mod[0], *weights, *tables, *tail)


def _flash_fast_kernel(q_ref, *refs):
    o_ref = refs[-1]
    tq = q_ref.shape[1]
    lanes = [slice(hd * MLA_PAD, (hd + 1) * MLA_PAD) for hd in range(2)]
    q = [q_ref[0, :, sl] for sl in lanes]
    acc = [jnp.zeros((tq, 2 * MLA_PAD), F32) for _ in range(2)]
    for k_ref, v_ref in zip(refs[0:-1:2], refs[1:-1:2]):
        for t in range(k_ref.shape[1] // MLA_KEY_TILE):
            keys = slice(t * MLA_KEY_TILE, (t + 1) * MLA_KEY_TILE)
            for hd in range(2):
                p = jnp.exp2(_dot_t(q[hd], k_ref[0, keys, lanes[hd]])).astype(BF16)
                acc[hd] = acc[hd] + _dot(p, v_ref[0, keys, :])
    for hd in range(2):
        own = slice(hd * MLA_V, (hd + 1) * MLA_V)
        o_ref[0, :, own] = (acc[hd][:, own] / acc[hd][:, MLA_PAD:MLA_PAD + 1]).astype(o_ref.dtype)


def _flash_fast(q, kv):
    B, Nq, HQ = q.shape
    pairs = HQ // (2 * MLA_PAD)
    tq = min(MLA_Q_ROWS, Nq)
    kv_specs = [pl.BlockSpec((1, a.shape[1], 2 * MLA_PAD), lambda b, h, i: (b, 0, h)) for pair in kv for a in pair]
    return pl.pallas_call(
        _flash_fast_kernel,
        grid=(B, pairs, Nq // tq),
        in_specs=[pl.BlockSpec((1, tq, 2 * MLA_PAD), lambda b, h, i: (b, i, h))] + kv_specs,
        out_specs=pl.BlockSpec((1, tq, 2 * MLA_V), lambda b, h, i: (b, i, h)),
        out_shape=jax.ShapeDtypeStruct((B, Nq, pairs * 2 * MLA_V), BF16),
        compiler_params=_cparams(3, VMEM_LARGE_MIB),
        name="mla_attention",
    )(q, *[a for pair in kv for a in pair])


def _flash_safe_kernel(q_ref, k_ref, v_ref, o_ref, *, tk):
    tq = q_ref.shape[1]
    nk = k_ref.shape[1] // tk
    for hd in range(2):
        lanes = slice(hd * MLA_PAD, (hd + 1) * MLA_PAD)
        q = q_ref[0, :, lanes]

        def body(j, carry):
            m, acc = carry
            start = pl.multiple_of(j * tk, tk)
            s = _dot_t(q, k_ref[0, pl.ds(start, tk), lanes])
            m_new = jnp.maximum(m, jnp.max(s, axis=-1, keepdims=True))
            p = jnp.exp2(s - m_new)
            acc = jnp.exp2(m - m_new) * acc + _dot(p.astype(BF16), v_ref[0, pl.ds(start, tk), :])
            return m_new, acc

        init = (jnp.full((tq, 1), NEG_BIG, F32), jnp.zeros((tq, 2 * MLA_PAD), F32))
        _, acc = lax.fori_loop(0, nk, body, init)
        own = slice(hd * MLA_V, (hd + 1) * MLA_V)
        o_ref[0, :, own] = (acc[:, own] / acc[:, MLA_PAD:MLA_PAD + 1]).astype(o_ref.dtype)


def _flash_safe(q, kv):
    k = jnp.concatenate([pair[0] for pair in kv], axis=1)
    v = jnp.concatenate([pair[1] for pair in kv], axis=1)
    B, Nq, HQ = q.shape
    Nk = k.shape[1]
    pairs = HQ // (2 * MLA_PAD)
    tq = min(MLA_Q_ROWS // 2, Nq)
    tk = next(t for t in (3 * MLA_KEY_TILE, 2 * MLA_KEY_TILE, MLA_KEY_TILE) if Nk % t == 0)
    kv_spec = pl.BlockSpec((1, Nk, 2 * MLA_PAD), lambda b, h, i: (b, 0, h))
    return pl.pallas_call(
        functools.partial(_flash_safe_kernel, tk=tk),
        grid=(B, pairs, Nq // tq),
        in_specs=[pl.BlockSpec((1, tq, 2 * MLA_PAD), lambda b, h, i: (b, i, h)), kv_spec, kv_spec],
        out_specs=pl.BlockSpec((1, tq, 2 * MLA_V), lambda b, h, i: (b, i, h)),
        out_shape=jax.ShapeDtypeStruct((B, Nq, pairs * 2 * MLA_V), BF16),
        compiler_params=_cparams(3, VMEM_LARGE_MIB),
        name="mla_attention_running_max",
    )(q, k, v)


def _mla_attention(q, kv, bound):
    return lax.cond(bound <= MLA_FAST_BOUND, lambda: _flash_fast(q, kv), lambda: _flash_safe(q, kv))


def _head_masks(width, head_dim, heads):
    lane = lax.broadcasted_iota(jnp.int32, (1, width), 1)
    return [(lane >= h * head_dim) & (lane < (h + 1) * head_dim) for h in range(heads)]


def _na_kernel(q_ref, k_ref, v_ref, kc_ref, vc_ref, bias_ref, bound_ref, o_ref, *, running_max):
    rows = k_ref.shape[1] // GRID_W
    groups = rows // NA_GROUP_ROWS
    tq = NA_GROUP_ROWS * GRID_W
    kc, vc = kc_ref[0], vc_ref[0]
    bound = bound_ref[...]
    masks = _head_masks(NA_WIDTH, NA_HEAD_DIM, NA_HEADS)
    for gi in range(q_ref.shape[1] // tq):
        g = pl.program_id(1) * (q_ref.shape[1] // tq) + gi
        variant = jnp.where(g == 0, 0, jnp.where(g == groups - 1, 2, 1))
        base = jnp.clip(g * NA_GROUP_ROWS - NA_WIN_R // 2, 0, rows - NA_KEY_ROWS)
        start = pl.multiple_of(base * GRID_W, GRID_W)
        kw = k_ref[0, pl.ds(start, NA_KEY_ROWS * GRID_W), :]
        vw = v_ref[0, pl.ds(start, NA_KEY_ROWS * GRID_W), :]
        q = q_ref[0, gi * tq:(gi + 1) * tq, :]
        out = jnp.zeros(q.shape, F32)
        for hd, mask in enumerate(masks):
            qh = jnp.where(mask, q, jnp.zeros_like(q))
            s_loc = _dot_t(qh, kw) + bias_ref[variant, hd]
            s_ctx = _dot_t(qh, kc) - bound
            if running_max:
                m = jnp.maximum(jnp.max(s_loc, axis=-1, keepdims=True), jnp.max(s_ctx, axis=-1, keepdims=True))
                s_loc, s_ctx = s_loc - m, s_ctx - m
            p_loc = jnp.exp2(s_loc)
            p_ctx = jnp.exp2(s_ctx)
            l = jnp.sum(p_loc, axis=-1, keepdims=True) + jnp.sum(p_ctx, axis=-1, keepdims=True)
            o = _dot(p_loc.astype(BF16), vw) + _dot(p_ctx.astype(BF16), vc)
            out = jnp.where(mask, o / l, out)
        o_ref[0, gi * tq:(gi + 1) * tq, :] = out.astype(o_ref.dtype)


def _na_call(q, k, v, kc, vc, bias, bound, running_max):
    B, S, W = q.shape
    C = kc.shape[1]
    tq = NA_GROUP_ROWS * GRID_W * NA_GROUPS_PER_STEP
    return pl.pallas_call(
        functools.partial(_na_kernel, running_max=running_max),
        grid=(B, S // tq),
        in_specs=[pl.BlockSpec((1, tq, W), lambda b, g: (b, g, 0)),
                  pl.BlockSpec((1, S, W), lambda b, g: (b, 0, 0)),
                  pl.BlockSpec((1, S, W), lambda b, g: (b, 0, 0)),
                  pl.BlockSpec((1, C, W), lambda b, g: (b, 0, 0)),
                  pl.BlockSpec((1, C, W), lambda b, g: (b, 0, 0)),
                  pl.BlockSpec(bias.shape, lambda b, g: (0, 0, 0, 0), pipeline_mode=pl.Buffered(1)),
                  pl.BlockSpec((1, 1), lambda b, g: (0, 0))],
        out_specs=pl.BlockSpec((1, tq, W), lambda b, g: (b, g, 0)),
        out_shape=jax.ShapeDtypeStruct((B, S, W), BF16),
        compiler_params=_cparams(2, VMEM_LARGE_MIB),
        name="na_attention_running_max" if running_max else "na_attention",
    )(q, k, v, kc, vc, bias, bound.reshape(1, 1))


def _na(q, k, v, kc, vc, rpb, g_nq, g_nk):
    qk = (LOG2E * NA_SCALE * NA_HEAD_DIM) * jnp.max(jnp.abs(g_nq)) * jnp.max(jnp.abs(g_nk))
    hi = LOG2E * jnp.maximum(jnp.max(rpb), 0.0)
    lo = LOG2E * jnp.minimum(jnp.min(rpb), 0.0)
    bound = qk + hi
    bias = _na_bias_tables(rpb, bound)
    args = (q, k, v, kc, vc, bias, bound)
    return lax.cond(2.0 * qk + hi - lo <= NA_FAST_SPAN,
                    lambda: _na_call(*args, running_max=False), lambda: _na_call(*args, running_max=True))


def _na_bias_tables(rpb, bound):
    n_c = 2 * NA_WIN_C - 1
    w = np.arange(GRID_W)[:, None]
    kcol = np.arange(GRID_W)[None, :]
    cs = np.clip(w - NA_WIN_C // 2, 0, GRID_W - NA_WIN_C)
    col_ok = (kcol >= cs) & (kcol < cs + NA_WIN_C)
    col_sel = (col_ok[..., None] & ((kcol - w + NA_WIN_C - 1)[..., None] == np.arange(n_c))).astype(np.float32)
    blocks = jnp.einsum("hrc,wkc->hrwk", rpb, col_sel, precision=lax.Precision.HIGHEST)
    blocks = jnp.where(col_ok, LOG2E * blocks - bound, NEG_BIG)
    masked = jnp.full((rpb.shape[0], GRID_W, GRID_W), NEG_BIG, F32)
    half = NA_WIN_R // 2
    variants = []
    for v in range(3):
        group = []
        for i in range(NA_GROUP_ROWS):
            q_rel, r0 = ((i, 0), (half + i, i), (NA_WIN_R + i, half))[v]
            row = [blocks[:, j - q_rel + NA_WIN_R - 1] if r0 <= j < r0 + NA_WIN_R else masked
                   for j in range(NA_KEY_ROWS)]
            group.append(jnp.concatenate(row, axis=-1))
        variants.append(jnp.concatenate(group, axis=1))
    return jnp.stack(variants).astype(F32)


def _small_attn_kernel(q_ref, k_ref, v_ref, o_ref):
    q, k, v = q_ref[0], k_ref[0], v_ref[0]
    out = jnp.zeros(q.shape, F32)
    for mask in _head_masks(NA_WIDTH, NA_HEAD_DIM, NA_HEADS):
        s = _dot_t(jnp.where(mask, q, jnp.zeros_like(q)), k)
        p = jnp.exp2(s - jnp.max(s, axis=-1, keepdims=True))
        o = _dot(p.astype(BF16), v)
        out = jnp.where(mask, o / jnp.sum(p, axis=-1, keepdims=True), out)
    o_ref[0] = out.astype(o_ref.dtype)


def _small_attn(q, k, v):
    B, N, W = q.shape
    spec = pl.BlockSpec((1, N, W), lambda b: (b, 0, 0))
    return pl.pallas_call(
        _small_attn_kernel, grid=(B,), in_specs=[spec, spec, spec], out_specs=spec,
        out_shape=jax.ShapeDtypeStruct((B, N, W), BF16),
        compiler_params=_cparams(1, VMEM_SMALL_MIB),
        name="ctx_na_attention",
    )(q, k, v)


def _mix_out_kernel(x_ref, gt_ref, ya_ref, yb_ref, u_ref, up_ref, un_ref, wo_ref, wp_ref, ps_ref, o_ref,
                    *, per_seq, seq_len):
    tm = x_ref.shape[0]
    i = pl.program_id(0) % per_seq
    prev = jnp.where(i == 0, 0.0, up_ref[...])
    nxt = jnp.where(i == per_seq - 1, 0.0, un_ref[...])
    ext = jnp.concatenate([prev, u_ref[...], nxt], axis=0)
    lane = lax.broadcasted_iota(jnp.int32, (1, POOL_WIDTH), 1)
    grp = lane // POOL_GROUP_DIM
    half = jnp.where(grp == 0, POOL_WINDOWS[0] // 2,
                     jnp.where(grp == 1, POOL_WINDOWS[1] // 2,
                               jnp.where(grp == 2, POOL_WINDOWS[2] // 2, POOL_WINDOWS[3] // 2)))
    sub = min(MIX_OUT_SUB_ROWS, tm)
    n = sub + 2 * POOL_HALO
    for s in range(tm // sub):
        rows = slice(s * sub, (s + 1) * sub)
        e = ext[s * sub:s * sub + n]
        mid = lambda w: w[POOL_HALO:POOL_HALO + sub]
        halves = []
        for c in range(2):
            ec = e[:, c * LANES:(c + 1) * LANES]
            p = ec + pltpu.roll(ec, 1, 0)
            levels = [p]
            for step in (1, 2, 4)[:2 * c + 1]:
                p = pltpu.roll(p, step, 0) + pltpu.roll(p, n - step, 0)
                levels.append(p)
            halves.append(jnp.where(lane[:, :LANES] < POOL_GROUP_DIM, mid(levels[-2]), mid(levels[-1])))
        win = jnp.concatenate(halves, axis=1)
        t = i * tm + s * sub + lax.broadcasted_iota(jnp.int32, (sub, 1), 0)
        cnt = jnp.minimum(t + half, seq_len) - jnp.maximum(t - half, 0)
        pooled = win / cnt.astype(F32) - mid(e)
        yc = _dot(pooled.astype(BF16), wp_ref[...]) * ps_ref[...]
        y = _dot(jnp.concatenate([ya_ref[rows, :], yb_ref[rows, :], yc.astype(BF16)], axis=1), wo_ref[...])
        o_ref[rows, :] = x_ref[rows, :] + gt_ref[0] * y


def _mix_out(x, mod, j_gate, ya, yb, u, lw, seq_len):
    T, D = x.shape
    nb = mod[2]
    tm = min(MIX_OUT_ROWS, seq_len)
    per = (T // nb) // tm
    per_seq = seq_len // tm
    hb = tm // POOL_HALO
    last_hb = T // POOL_HALO - 1
    tok = lambda w: pl.BlockSpec((tm, w), lambda i: (i, 0))
    full = _resident
    return pl.pallas_call(
        functools.partial(_mix_out_kernel, per_seq=per_seq, seq_len=seq_len),
        grid=(T // tm,),
        in_specs=[tok(D), _mod_spec(mod, j_gate, per),
                  tok(MLA_WIDTH), tok(NA_WIDTH), tok(POOL_WIDTH),
                  pl.BlockSpec((POOL_HALO, POOL_WIDTH), lambda i: (jnp.maximum(i * hb - 1, 0), 0)),
                  pl.BlockSpec((POOL_HALO, POOL_WIDTH), lambda i: (jnp.minimum((i + 1) * hb, last_hb), 0)),
                  full(lw["w_out"]), full(lw["w_pool"]), full(lw["pool_scale"])],
        out_specs=tok(D),
        out_shape=jax.ShapeDtypeStruct((T, D), F32),
        compiler_params=_cparams(1, VMEM_MEDIUM_MIB),
        name="mix_out",
    )(x, mod[0], ya, yb, u, u, u, lw["w_out"], lw["w_pool"], lw["pool_scale"])


def _rope_partner(a):
    lead = a.shape[:-1]
    return a.reshape(lead + (2, 2, MLA_ROPE // 4))[..., ::-1, :].reshape(lead + (MLA_ROPE,))


def _rope_tables(n_tok):
    t = np.arange(n_tok)
    axis_dim = MLA_ROPE // 2
    inv_freq = ROPE_BASE ** (-np.arange(0, axis_dim, 2, dtype=np.float64) / axis_dim)
    ar = (t // GRID_W)[:, None] * inv_freq
    ac = (t % GRID_W)[:, None] * inv_freq
    ones = np.ones((n_tok, MLA_NOPE))
    zeros = np.zeros((n_tok, MLA_NOPE))
    pad = np.zeros((n_tok, MLA_PAD - MLA_QK))
    cos = np.concatenate([ones, np.cos(ar), np.cos(ar), np.cos(ac), np.cos(ac), pad + 1.0], axis=1)
    sin = np.concatenate([zeros, -np.sin(ar), np.sin(ar), -np.sin(ac), np.sin(ac), pad], axis=1)
    return jnp.asarray(cos, F32), jnp.asarray(sin, F32)


def _identity_tables(n_tok):
    return jnp.ones((n_tok, MLA_PAD), F32), jnp.zeros((n_tok, MLA_PAD), F32)


def _layer_weights(l, w_in, g_cq, w_q_up, g_ckv, w_kv_up, g_mla_q, g_mla_k, g_na_q, g_na_k,
                   w_pool, pool_scale, w_out):
    D = w_in.shape[1]
    wi = w_in[l]
    w_kr = wi[:, COL_KR:COL_KR + MLA_ROPE]
    in_rope_lanes = lambda w: jnp.pad(w, ((0, 0), (MLA_NOPE, MLA_PAD - MLA_QK)))
    w_in_ext = jnp.concatenate([wi[:, :COL_KR], in_rope_lanes(w_kr), in_rope_lanes(_rope_partner(w_kr)),
                                wi[:, COL_KR + MLA_ROPE:]], axis=1).astype(BF16)
    wq = w_q_up[l].reshape(MLA_Q_RANK, MLA_HEADS, MLA_QK)
    wq = jnp.concatenate([wq, _rope_partner(wq[:, :, MLA_NOPE:])], axis=2).reshape(MLA_Q_RANK, MLA_HEADS * MLA_PAD)
    wkv = w_kv_up[l].reshape(MLA_KV_RANK, MLA_HEADS, MLA_NOPE + MLA_V)
    wk = jnp.pad(wkv[:, :, :MLA_NOPE], ((0, 0), (0, 0), (0, MLA_PAD - MLA_NOPE)))
    wv = wkv[:, :, MLA_NOPE:]
    w_kv = jnp.concatenate([wk.reshape(MLA_KV_RANK, -1), wv.reshape(MLA_KV_RANK, -1)], axis=1)
    pad_gain = lambda g: jnp.pad(g, (0, MLA_PAD - MLA_QK)).reshape(1, MLA_PAD)
    partner_gain = lambda g: jnp.pad(_rope_partner(g[MLA_NOPE:]), (MLA_NOPE, MLA_PAD - MLA_QK)).reshape(1, MLA_PAD)
    head_id = np.arange(NA_WIDTH) // NA_HEAD_DIM
    bd = jnp.asarray((head_id[:, None] == head_id[None, :]) / NA_HEAD_DIM, BF16)
    wp = jnp.zeros((POOL_WIDTH, POOL_WIDTH), F32)
    for g in range(len(POOL_WINDOWS)):
        sl = slice(g * POOL_GROUP_DIM, (g + 1) * POOL_GROUP_DIM)
        wp = wp.at[sl, sl].set(w_pool[l, g])
    return {
        "w_in": w_in_ext, "g_cq": g_cq[l].reshape(1, -1), "w_q": wq.astype(BF16),
        "g_ckv": g_ckv[l].reshape(1, -1), "w_kv": w_kv.astype(BF16),
        "g_q": pad_gain(g_mla_q[l]), "g_k": pad_gain(g_mla_k[l]),
        "g_q_partner": partner_gain(g_mla_q[l]), "g_k_partner": partner_gain(g_mla_k[l]),
        "bd": bd, "g_nq": jnp.tile(g_na_q[l], NA_HEADS).reshape(1, -1), "g_nk": jnp.tile(g_na_k[l], NA_HEADS).reshape(1, -1),
        "w_pool": wp.astype(BF16), "pool_scale": pool_scale[l].reshape(1, -1), "w_out": w_out[l].astype(BF16),
    }


def kernel(x, c, ctx, c_ctx, w_mod, b_mod, w_ffn_gate, w_ffn_up, w_ffn_down, w_in, g_cq, w_q_up, g_ckv, w_kv_up,
           g_mla_q, g_mla_k, g_na_q, g_na_k, na_rpb, w_pool, pool_scale, w_out):
    B, S, D = x.shape
    C = ctx.shape[1]
    L = w_mod.shape[0]
    assert S % (NA_GROUPS_PER_STEP * NA_GROUP_ROWS * GRID_W) == 0 and S // GRID_W >= NA_KEY_ROWS and B + 1 <= SUBLANES

    cc = jnp.zeros((SUBLANES, D), F32).at[:B].set(c).at[B].set(c_ctx)
    mods = _modulation(cc, w_mod, b_mod).reshape(L * SUBLANES * N_MOD, 1, D)
    rope = _rope_tables(S)
    no_rope = _identity_tables(C)
    ffn_w = (w_ffn_gate.astype(BF16), w_ffn_up.astype(BF16), w_ffn_down.astype(BF16))

    xs = x.reshape(B * S, D)
    xc = ctx.reshape(B * C, D)
    for l in range(L):
        last = l == L - 1
        mx = (mods, l * SUBLANES * N_MOD, B)
        mc = (mods, (l * SUBLANES + B) * N_MOD, 1)
        lw = _layer_weights(l, w_in, g_cq, w_q_up, g_ckv, w_kv_up, g_mla_q, g_mla_k, g_na_q, g_na_k,
                            w_pool, pool_scale, w_out)
        xs = _ffn(xs, mx, (0, 1, 2), ffn_w, l, 0)
        xc = _ffn(xc, mc, (0, 1, 2), ffn_w, l, 0)

        qx, kx, vx, nqx, nkx, nvx, ux = _pre(xs, mx, (3, 4), lw, rope, S)
        qc, kc, vc, nqc, nkc, nvc, uc = _pre(xc, mc, (3, 4), lw, no_rope, C)
        per_b = lambda a, n: a.reshape(B, n, a.shape[-1])
        bound = _mla_score_bound(lw["g_q"], lw["g_k"])[0, 0]
        kv_c = (per_b(kc, C), per_b(vc, C))
        ya = _mla_attention(per_b(qx, S), [kv_c, (per_b(kx, S), per_b(vx, S))], bound)
        yb = _na(per_b(nqx, S), per_b(nkx, S), per_b(nvx, S), per_b(nkc, C), per_b(nvc, C),
                 na_rpb[l], g_na_q[l], g_na_k[l])
        xs = _mix_out(xs, mx, 5, ya.reshape(B * S, -1), yb.reshape(B * S, -1), ux, lw, S)
        xs = _ffn(xs, mx, (6, 7, 8), ffn_w, l, 1)
        if not last:
            ya_c = _mla_attention(per_b(qc, C), [kv_c], bound)
            yb_c = _small_attn(per_b(nqc, C), per_b(nkc, C), per_b(nvc, C))
            xc = _mix_out(xc, mc, 5, ya_c.reshape(B * C, -1), yb_c.reshape(B * C, -1), uc, lw, C)
            xc = _ffn(xc, mc, (6, 7, 8), ffn_w, l, 1)
    return xs.reshape(B, S, D)
```
